```python
import math
import jax, jax.numpy as jnp
from jax import lax
import numpy as np

D_MODEL = 1024
BATCH = 8
SEQ = 8192
DEPTH = 2

CHUNK = 64
N_A_LAYERS = DEPTH // 2
N_B_LAYERS = DEPTH - N_A_LAYERS
N_DENSE_LAYERS = (DEPTH + 1) // 2
N_MOE_LAYERS = DEPTH // 2
A_HEADS = 8
A_HEAD_DIM = D_MODEL // A_HEADS
A_WIDTH = A_HEADS * A_HEAD_DIM
A_CONV = 4
A_IN_COLS = 4 * A_WIDTH + 2 * A_HEADS
B_Q_HEADS = 16
B_KV_HEADS = 4
B_HEAD_DIM = 64
B_GROUP = B_Q_HEADS // B_KV_HEADS
WINDOW = 128
WINDOW_CHUNKS = WINDOW // CHUNK
BAND = (WINDOW_CHUNKS + 1) * CHUNK
REL_BUCKETS = 32
REL_MAX_DIST = 128
D_FF = 2816
N_EXPERTS = 8
TOP_K = 2
D_FF_EXPERT = 3584
MOE_BLOCK = 256
EPS = 1e-6

kernel_name = "hybrid_yoco_deltanet_swa_sink_moe"


def rms_norm(x, w):
    xf = x.astype(jnp.float32)
    y = xf * lax.rsqrt(jnp.mean(xf * xf, axis=-1, keepdims=True) + EPS)
    return (y * w.astype(jnp.float32)).astype(x.dtype)


def l2_norm(x):
    xf = x.astype(jnp.float32)
    return xf * lax.rsqrt(jnp.sum(xf * xf, axis=-1, keepdims=True) + EPS)


def causal_depthwise_conv(x, w):
    c = x.shape[-1]
    return lax.conv_general_dilated(
        x, w.astype(x.dtype)[:, None, :], window_strides=(1,),
        padding=((A_CONV - 1, 0),), dimension_numbers=("NWC", "WIO", "NWC"),
        feature_group_count=c)


def gated_delta_rule_chunked(q, k, v, beta, g):
    bsz, L, H, dk = q.shape
    dv = v.shape[-1]
    n = L // CHUNK
    f32 = jnp.float32

    def to_chunks(t):
        return t.astype(f32).reshape(bsz, n, CHUNK, H, -1).transpose(1, 0, 3, 2, 4)

    q, k, v = to_chunks(q), to_chunks(k), to_chunks(v)
    beta = beta.astype(f32).reshape(bsz, n, CHUNK, H).transpose(1, 0, 3, 2)
    g = jnp.cumsum(g.astype(f32).reshape(bsz, n, CHUNK, H).transpose(1, 0, 3, 2), axis=-1)

    tri = jnp.tril(jnp.ones((CHUNK, CHUNK), bool))
    strict = jnp.tril(jnp.ones((CHUNK, CHUNK), bool), -1)
    decay = jnp.exp(jnp.where(tri, g[..., :, None] - g[..., None, :], -jnp.inf))
    k_beta = k * beta[..., None]
    lower = jnp.where(strict, jnp.einsum('nbhid,nbhjd->nbhij', k_beta, k) * decay, 0.0)
    eye = jnp.eye(CHUNK, dtype=f32)
    t_mat = lax.linalg.triangular_solve(eye + lower, jnp.broadcast_to(eye, lower.shape),
                                        left_side=True, lower=True, unit_diagonal=True)
    u = jnp.einsum('nbhij,nbhjd->nbhid', t_mat, v * beta[..., None])
    w = jnp.einsum('nbhij,nbhjd->nbhid', t_mat, k_beta * jnp.exp(g)[..., None])
    attn = jnp.einsum('nbhid,nbhjd->nbhij', q, k) * decay
    g_last = g[..., -1]
    q_dec = q * jnp.exp(g)[..., None]
    k_dec = k * jnp.exp(g_last[..., None] - g)[..., None]

    def step(state, inp):
        qd, kd, u_c, w_c, a_c, gl = inp
        v_new = u_c - jnp.einsum('bhcd,bhde->bhce', w_c, state)
        o_c = jnp.einsum('bhcd,bhde->bhce', qd, state) + jnp.einsum('bhij,bhje->bhie', a_c, v_new)
        state = state * jnp.exp(gl)[..., None, None] + jnp.einsum('bhcd,bhce->bhde', kd, v_new)
        return state, o_c

    s0 = jnp.zeros((bsz, H, dk, dv), f32)
    _, o = lax.scan(step, s0, (q_dec, k_dec, u, w, attn, g_last))
    return o.transpose(1, 0, 3, 2, 4).reshape(bsz, L, H, dv)


def gated_deltanet(hn, w_in, conv_w, a_log, dt_bias, onorm_w, w_out):
    bsz, L, _ = hn.shape
    proj = hn @ w_in
    qkv = jax.nn.silu(causal_depthwise_conv(proj[..., :3 * A_WIDTH], conv_w))
    gate = proj[..., 3 * A_WIDTH:4 * A_WIDTH]
    b_logit = proj[..., 4 * A_WIDTH:4 * A_WIDTH + A_HEADS]
    a_logit = proj[..., 4 * A_WIDTH + A_HEADS:]
    shp = (bsz, L, A_HEADS, A_HEAD_DIM)
    q = l2_norm(qkv[..., :A_WIDTH].reshape(shp)) * (A_HEAD_DIM ** -0.5)
    k = l2_norm(qkv[..., A_WIDTH:2 * A_WIDTH].reshape(shp))
    v = qkv[..., 2 * A_WIDTH:].reshape(shp)
    beta = jax.nn.sigmoid(b_logit.astype(jnp.float32))
    g = -jnp.exp(a_log.astype(jnp.float32)) * jax.nn.softplus(
        a_logit.astype(jnp.float32) + dt_bias.astype(jnp.float32))
    o = gated_delta_rule_chunked(q, k, v, beta, g).astype(hn.dtype)
    o = rms_norm(o, onorm_w) * jax.nn.silu(gate.reshape(shp))
    return o.reshape(bsz, L, A_WIDTH) @ w_out


def shared_kv_band(h, kv_norm_w, w_kv):
    bsz, L, _ = h.shape
    n = L // CHUNK
    kv = rms_norm(h, kv_norm_w) @ w_kv
    half = B_KV_HEADS * B_HEAD_DIM

    def band(t):
        t = t.reshape(bsz, n, CHUNK, B_KV_HEADS, B_HEAD_DIM)
        tp = jnp.pad(t, ((0, 0), (WINDOW_CHUNKS, 0), (0, 0), (0, 0), (0, 0)))
        return jnp.concatenate([tp[:, i:i + n] for i in range(WINDOW_CHUNKS + 1)], axis=2)

    return band(kv[..., :half]), band(kv[..., half:])


def t5_buckets(rel):
    nb = REL_BUCKETS // 2
    max_exact = nb // 2
    ret = jnp.where(rel > 0, nb, 0)
    dist = jnp.abs(rel)
    dist_f = jnp.maximum(dist, 1).astype(jnp.float32)
    large = max_exact + (jnp.log(dist_f / max_exact) / math.log(REL_MAX_DIST / max_exact)
                         * (nb - max_exact)).astype(jnp.int32)
    large = jnp.minimum(large, nb - 1)
    return ret + jnp.where(dist < max_exact, dist, large)


def swa_sink_attention(hn, w_q, sinks, w_o, k_band, v_band, bias):
    bsz, L, _ = hn.shape
    n = L // CHUNK
    q = (hn @ w_q).reshape(bsz, n, CHUNK, B_KV_HEADS, B_GROUP, B_HEAD_DIM) * (B_HEAD_DIM ** -0.5)
    s = jnp.einsum('bnqhgd,bnkhd->bnhgqk', q, k_band).astype(jnp.float32) + bias
    key_chunk = jnp.arange(n)[:, None] - WINDOW_CHUNKS + jnp.arange(BAND)[None, :] // CHUNK
    valid = (key_chunk >= 0)[None, :, None, None, None, :]
    s = jnp.where(valid, s, -jnp.inf)
    sink = sinks.astype(jnp.float32).reshape(B_KV_HEADS, B_GROUP)[:, :, None, None]
    m = jnp.maximum(jnp.max(s, axis=-1, keepdims=True), sink)
    p = jnp.exp(s - m)
    p = p / (jnp.sum(p, axis=-1, keepdims=True) + jnp.exp(sink - m))
    o = jnp.einsum('bnhgqk,bnkhd->bnqhgd', p.astype(v_band.dtype), v_band)
    return o.reshape(bsz, L, B_Q_HEADS * B_HEAD_DIM) @ w_o


def swiglu(hn, w_gate, w_up, w_down):
    return (jax.nn.silu(hn @ w_gate) * (hn @ w_up)) @ w_down


def moe_swiglu(hn, router_w, w_gate, w_up, w_down):
    bsz, L, d = hn.shape
    t = bsz * L
    a = t * TOP_K
    xt = hn.reshape(t, d)
    logits = (xt @ router_w).astype(jnp.float32)
    top_logit, top_idx = lax.top_k(logits, TOP_K)
    gate = jax.nn.softmax(top_logit, axis=-1)
    e_flat = top_idx.reshape(-1)
    tok_flat = jnp.arange(a, dtype=jnp.int32) // TOP_K
    w_flat = gate.reshape(-1)
    order = jnp.argsort(e_flat)
    se = e_flat[order]
    counts = jnp.bincount(e_flat, length=N_EXPERTS)
    g_start = jnp.cumsum(counts) - counts
    p_counts = (counts + MOE_BLOCK - 1) // MOE_BLOCK * MOE_BLOCK
    p_end = jnp.cumsum(p_counts)
    p_start = p_end - p_counts
    dest = p_start[se] + jnp.arange(a) - g_start[se]
    n_blocks = -(-a // MOE_BLOCK) + N_EXPERTS
    n_slots = n_blocks * MOE_BLOCK
    slot_tok = jnp.full((n_slots,), t, jnp.int32).at[dest].set(tok_flat[order])
    slot_w = jnp.zeros((n_slots,), jnp.float32).at[dest].set(w_flat[order])
    blk_e = jnp.minimum(jnp.searchsorted(p_end, jnp.arange(n_blocks) * MOE_BLOCK, side='right'),
                        N_EXPERTS - 1)
    x_pad = jnp.concatenate([xt, jnp.zeros((1, d), xt.dtype)], axis=0)
    xb = x_pad[slot_tok].reshape(n_blocks, MOE_BLOCK, d)

    def expert_block(args):
        xblk, e = args
        return (jax.nn.silu(xblk @ w_gate[e]) * (xblk @ w_up[e])) @ w_down[e]

    yb = lax.map(expert_block, (xb, blk_e)).reshape(n_slots, d)
    out = jnp.zeros((t + 1, d), jnp.float32).at[slot_tok].add(yb.astype(jnp.float32) * slot_w[:, None])
    return out[:t].astype(hn.dtype).reshape(bsz, L, d)


def setup_inputs(seed: int = 0) -> dict:
    key = jax.random.key(seed)
    ks = iter(jax.random.split(key, 32))
    f32 = jnp.float32

    def nrm(shape, scale):
        return jax.random.normal(next(ks), shape, f32) * scale

    def gain(shape):
        return 1.0 + nrm(shape, 0.02)

    dt = jnp.exp(jax.random.uniform(next(ks), (N_A_LAYERS, A_HEADS), f32,
                                    math.log(1e-3), math.log(1e-1)))
    return {
        "x": nrm((BATCH, SEQ, D_MODEL), 1.0),
        "attn_norm_w": gain((DEPTH, D_MODEL)),
        "ffn_norm_w": gain((DEPTH, D_MODEL)),
        "a_w_in": nrm((N_A_LAYERS, D_MODEL, A_IN_COLS), D_MODEL ** -0.5),
        "a_conv_w": nrm((N_A_LAYERS, A_CONV, 3 * A_WIDTH), A_CONV ** -0.5),
        "a_a_log": jnp.log(jax.random.uniform(next(ks), (N_A_LAYERS, A_HEADS), f32, 1.0, 16.0)),
        "a_dt_bias": dt + jnp.log(-jnp.expm1(-dt)),
        "a_onorm_w": gain((N_A_LAYERS, A_HEAD_DIM)),
        "a_w_out": nrm((N_A_LAYERS, A_WIDTH, D_MODEL), A_WIDTH ** -0.5),
        "kv_norm_w": gain((D_MODEL,)),
        "w_kv": nrm((D_MODEL, 2 * B_KV_HEADS * B_HEAD_DIM), D_MODEL ** -0.5),
        "b_w_q": nrm((N_B_LAYERS, D_MODEL, B_Q_HEADS * B_HEAD_DIM), D_MODEL ** -0.5),
        "b_sinks": nrm((N_B_LAYERS, B_Q_HEADS), 0.5),
        "b_w_o": nrm((N_B_LAYERS, B_Q_HEADS * B_HEAD_DIM, D_MODEL), (B_Q_HEADS * B_HEAD_DIM) ** -0.5),
        "rel_bias": nrm((REL_BUCKETS, B_Q_HEADS), 0.5),
        "ffn_w_gate": nrm((N_DENSE_LAYERS, D_MODEL, D_FF), D_MODEL ** -0.5),
        "ffn_w_up": nrm((N_DENSE_LAYERS, D_MODEL, D_FF), D_MODEL ** -0.5),
        "ffn_w_down": nrm((N_DENSE_LAYERS, D_FF, D_MODEL), D_FF ** -0.5),
        "moe_router": nrm((N_MOE_LAYERS, D_MODEL, N_EXPERTS), D_MODEL ** -0.5),
        "moe_w_gate": nrm((N_MOE_LAYERS, N_EXPERTS, D_MODEL, D_FF_EXPERT), D_MODEL ** -0.5),
        "moe_w_up": nrm((N_MOE_LAYERS, N_EXPERTS, D_MODEL, D_FF_EXPERT), D_MODEL ** -0.5),
        "moe_w_down": nrm((N_MOE_LAYERS, N_EXPERTS, D_FF_EXPERT, D_MODEL), D_FF_EXPERT ** -0.5),
        "final_norm_w": gain((D_MODEL,)),
    }


def reference(x, attn_norm_w, ffn_norm_w, a_w_in, a_conv_w, a_a_log, a_dt_bias, a_onorm_w,
              a_w_out, kv_norm_w, w_kv, b_w_q, b_sinks, b_w_o, rel_bias, ffn_w_gate, ffn_w_up,
              ffn_w_down, moe_router, moe_w_gate, moe_w_up, moe_w_down, final_norm_w):
    rel = (jnp.arange(BAND)[None, :] - WINDOW_CHUNKS * CHUNK) - jnp.arange(CHUNK)[:, None]
    bias = rel_bias[t5_buckets(rel)].astype(jnp.float32)
    bias = bias.transpose(2, 0, 1).reshape(B_KV_HEADS, B_GROUP, CHUNK, BAND)

    h = x
    k_band = v_band = None
    for i in range(DEPTH):
        hn = rms_norm(h, attn_norm_w[i])
        if i < N_A_LAYERS:
            h = h + gated_deltanet(hn, a_w_in[i], a_conv_w[i], a_a_log[i], a_dt_bias[i],
                                   a_onorm_w[i], a_w_out[i])
        else:
            j = i - N_A_LAYERS
            h = h + swa_sink_attention(hn, b_w_q[j], b_sinks[j], b_w_o[j], k_band, v_band, bias)
        hn = rms_norm(h, ffn_norm_w[i])
        if i % 2 == 0:
            h = h + swiglu(hn, ffn_w_gate[i // 2], ffn_w_up[i // 2], ffn_w_down[i // 2])
        else:
            h = h + moe_swiglu(hn, moe_router[i // 2], moe_w_gate[i // 2], moe_w_up[i // 2],
                               moe_w_down[i // 2])
        if i == N_A_LAYERS - 1:
            k_band, v_band = shared_kv_band(h, kv_norm_w, w_kv)
    return rms_norm(h, final_norm_w)
```

```python
import functools
import math

import numpy as np
import jax
import jax.numpy as jnp
from jax import lax
from jax.experimental import pallas as pl
from jax.experimental.pallas import tpu as pltpu

F32 = jnp.float32
BF16 = jnp.bfloat16
I32 = jnp.int32

EPS = 1e-6
CHUNK = 64
A_HEADS = 8
A_HEAD_DIM = 128
A_WIDTH = A_HEADS * A_HEAD_DIM
A_CONV = 4
B_Q_HEADS = 16
B_KV_HEADS = 4
B_HEAD_DIM = 64
B_GROUP = B_Q_HEADS // B_KV_HEADS
WINDOW_CHUNKS = 2
BAND = (WINDOW_CHUNKS + 1) * CHUNK
REL_BUCKETS = 32
REL_MAX_DIST = 128
N_EXPERTS = 8
LANES = 128

VMEM_LIMIT = 56 * 1024 * 1024

ROW_TILE = 512
CONV_COLS = 512
FFN_COLS = 256
MOE_ROWS = 512
MOE_COLS = 512
DMA_ROWS = 512


def _tile(n, pref):
    t = min(n, pref)
    assert n % t == 0, (n, t)
    return t


def _mm(a, b):
    return jnp.dot(a.astype(BF16), b.astype(BF16), preferred_element_type=F32)


def _mm_nt(a, b):
    return lax.dot_general(a.astype(BF16), b.astype(BF16), (((1,), (1,)), ((), ())),
                           preferred_element_type=F32)


def _mm_tn(a, b):
    return lax.dot_general(a.astype(BF16), b.astype(BF16), (((0,), (0,)), ((), ())),
                           preferred_element_type=F32)


def _silu(x):
    return x * jax.nn.sigmoid(x)


def _rms(x, w):
    return x * lax.rsqrt(jnp.mean(x * x, axis=-1, keepdims=True) + EPS) * w


def _const_spec(shape):
    nd = len(shape)
    return pl.BlockSpec(shape, lambda *_: (0,) * nd, pipeline_mode=pl.Buffered(1))


def _params(sem):
    return pltpu.CompilerParams(dimension_semantics=sem, vmem_limit_bytes=VMEM_LIMIT)


def _in_kernel(x_ref, nw_ref, w_ref, wba_ref, cw_ref, ap_ref, seg_ref,
               q_ref, k_ref, v_ref, gate_ref, bg_ref, gct_ref, carry_ref, ext_ref, *, tm):
    @pl.when(pl.program_id(1) == 0)
    def _():
        carry_ref[...] = jnp.zeros_like(carry_ref)

    hn = _rms(x_ref[0], nw_ref[...]).astype(BF16)
    outs = (q_ref, k_ref, v_ref)
    for c in range(3 * A_WIDTH // CONV_COLS):
        cs = slice(c * CONV_COLS, (c + 1) * CONV_COLS)
        p = jnp.dot(hn, w_ref[:, cs], preferred_element_type=F32)
        ext_ref[0:8, :] = carry_ref[:, cs]
        ext_ref[8:tm + 8, :] = p
        carry_ref[:, cs] = p[tm - 8:, :]
        acc = cw_ref[A_CONV - 1:A_CONV, cs] * p
        for j in range(A_CONV - 1):
            acc = acc + cw_ref[j:j + 1, cs] * ext_ref[5 + j:5 + j + tm, :]
        a = _silu(acc)
        which, off = divmod(c * CONV_COLS, A_WIDTH)
        for hh in range(CONV_COLS // A_HEAD_DIM):
            hs = a[:, hh * A_HEAD_DIM:(hh + 1) * A_HEAD_DIM]
            if which < 2:
                hs = hs * lax.rsqrt(jnp.sum(hs * hs, axis=-1, keepdims=True) + EPS)
            if which == 0:
                hs = hs * (A_HEAD_DIM ** -0.5)
            lo = off + hh * A_HEAD_DIM
            outs[which][0, :, lo:lo + A_HEAD_DIM] = hs.astype(BF16)

    for c in range(A_WIDTH // CONV_COLS):
        cs = slice(3 * A_WIDTH + c * CONV_COLS, 3 * A_WIDTH + (c + 1) * CONV_COLS)
        gate_ref[0, :, c * CONV_COLS:(c + 1) * CONV_COLS] = jnp.dot(
            hn, w_ref[:, cs], preferred_element_type=F32)

    ba = jnp.dot(hn, wba_ref[...], preferred_element_type=F32)
    beta = jax.nn.sigmoid(ba)
    z = ba + ap_ref[1:2, :]
    softplus = jnp.maximum(z, 0.0) + jnp.log1p(jnp.exp(-jnp.abs(z)))
    g = -jnp.exp(ap_ref[0:1, :]) * softplus
    gc = jnp.dot(seg_ref[...], g, precision=lax.Precision.HIGHEST, preferred_element_type=F32)
    lane = lax.broadcasted_iota(I32, (tm, LANES), 1)
    bg = jnp.where(lane < A_HEADS, beta, gc)
    bg_ref[0] = bg
    bgt = bg.T
    for ci in range(tm // CHUNK):
        gct_ref[0, ci] = bgt[0:2 * A_HEADS, ci * CHUNK:(ci + 1) * CHUNK]


def _stage_in(x, norm_w, w_in, conv_w, a_log, dt_bias):
    bsz, L, d = x.shape
    tm = _tile(L, ROW_TILE)
    nc = L // CHUNK
    w_main = w_in[:, :4 * A_WIDTH].astype(BF16)
    w_ba = jnp.pad(w_in[:, 4 * A_WIDTH:], ((0, 0), (0, LANES - 2 * A_HEADS))).astype(BF16)
    ap = jnp.zeros((2, LANES), F32)
    ap = ap.at[0, A_HEADS:2 * A_HEADS].set(a_log.astype(F32))
    ap = ap.at[1, A_HEADS:2 * A_HEADS].set(dt_bias.astype(F32))
    r = np.arange(tm)
    seg = jnp.asarray(((r[:, None] >= r[None, :]) &
                       (r[:, None] // CHUNK == r[None, :] // CHUNK)).astype(np.float32))
    row_spec = lambda w: pl.BlockSpec((1, tm, w), lambda b, i: (b, i, 0))
    act = jax.ShapeDtypeStruct((bsz, L, A_WIDTH), BF16)
    return pl.pallas_call(
        functools.partial(_in_kernel, tm=tm),
        grid=(bsz, L // tm),
        in_specs=[row_spec(d), _const_spec((1, d)), _const_spec((d, 4 * A_WIDTH)),
                  _const_spec((d, LANES)), _const_spec((A_CONV, 3 * A_WIDTH)),
                  _const_spec((2, LANES)), _const_spec((tm, tm))],
        out_specs=[row_spec(A_WIDTH), row_spec(A_WIDTH), row_spec(A_WIDTH), row_spec(A_WIDTH),
                   row_spec(LANES),
                   pl.BlockSpec((1, tm // CHUNK, 2 * A_HEADS, CHUNK), lambda b, i: (b, i, 0, 0))],
        out_shape=[act, act, act, jax.ShapeDtypeStruct((bsz, L, A_WIDTH), F32),
                   jax.ShapeDtypeStruct((bsz, L, LANES), F32),
                   jax.ShapeDtypeStruct((bsz, nc, 2 * A_HEADS, CHUNK), F32)],
        scratch_shapes=[pltpu.VMEM((8, 3 * A_WIDTH), F32), pltpu.VMEM((tm + 8, CONV_COLS), F32)],
        compiler_params=_params(("arbitrary", "arbitrary")),
        name="in_proj",
    )(x, norm_w.reshape(1, d), w_main, w_ba, conv_w.astype(F32), ap, seg)


def _unit_lower_inverse(lower, row, col):
    eye = (row == col).astype(F32)
    diag = (row >> 4) == (col >> 4)
    ld = jnp.where(diag, lower, 0.0)
    lo = lower - ld
    p = eye - ld
    s = _mm(ld, ld)
    p = p + _mm(p, s)
    s = _mm(s, s)
    p = p + _mm(p, s)
    s = _mm(s, s)
    td = p + _mm(p, s)
    m = _mm(td, lo)
    n = eye - m
    n = n + _mm(n, _mm(m, m))
    return _mm(n, td)


def _delta_kernel(q_ref, k_ref, v_ref, gate_ref, bg_ref, gct_ref, ow_ref, o_ref, s_ref, *, cb):
    @pl.when(pl.program_id(1) == 0)
    def _():
        s_ref[...] = jnp.zeros_like(s_ref)

    row = lax.broadcasted_iota(I32, (CHUNK, CHUNK), 0)
    col = lax.broadcasted_iota(I32, (CHUNK, CHUNK), 1)
    tri = row >= col
    strict = row > col

    def chunk(c, carry):
        rows = pl.ds(pl.multiple_of(c * CHUNK, CHUNK), CHUNK)
        for h in range(A_HEADS):
            cols = slice(h * A_HEAD_DIM, (h + 1) * A_HEAD_DIM)
            q = q_ref[0, rows, cols].astype(F32)
            k = k_ref[0, rows, cols].astype(F32)
            v = v_ref[0, rows, cols].astype(F32)
            beta = bg_ref[0, rows, h:h + 1]
            gcol = bg_ref[0, rows, A_HEADS + h:A_HEADS + h + 1]
            grow = gct_ref[0, c, A_HEADS + h:A_HEADS + h + 1, :]
            glast = grow[:, CHUNK - 1:CHUNK]
            decay = jnp.exp(jnp.where(tri, gcol - grow, -jnp.inf))
            eg = jnp.exp(gcol)
            kb = k * beta
            lower = jnp.where(strict, _mm_nt(kb, k) * decay, 0.0)
            t = _unit_lower_inverse(lower, row, col)
            uw = _mm(t, jnp.concatenate([v * beta, kb * eg], axis=1))
            u = uw[:, :A_HEAD_DIM]
            w = uw[:, A_HEAD_DIM:]
            attn = _mm_nt(q, k) * decay
            kd = k * jnp.exp(glast - gcol)
            state = s_ref[h]
            ws_qs = _mm(jnp.concatenate([w, q * eg], axis=0), state)
            vnew = u - ws_qs[:CHUNK]
            o = ws_qs[CHUNK:] + _mm(attn, vnew)
            s_ref[h] = state * jnp.exp(glast) + _mm_tn(kd, vnew)
            o = _rms(o, ow_ref[...]) * _silu(gate_ref[0, rows, cols])
            o_ref[0, rows, cols] = o.astype(BF16)
        return carry

    lax.fori_loop(0, cb, chunk, 0)


def _stage_delta(q, k, v, gate, bg, gct, onorm_w):
    bsz, L, _ = q.shape
    tm = _tile(L, ROW_TILE)
    cb = tm // CHUNK
    row_spec = lambda w: pl.BlockSpec((1, tm, w), lambda b, i: (b, i, 0))
    return pl.pallas_call(
        functools.partial(_delta_kernel, cb=cb),
        grid=(bsz, L // tm),
        in_specs=[row_spec(A_WIDTH), row_spec(A_WIDTH), row_spec(A_WIDTH), row_spec(A_WIDTH),
                  row_spec(LANES),
                  pl.BlockSpec((1, cb, 2 * A_HEADS, CHUNK), lambda b, i: (b, i, 0, 0)),
                  _const_spec((1, A_HEAD_DIM))],
        out_specs=row_spec(A_WIDTH),
        out_shape=jax.ShapeDtypeStruct((bsz, L, A_WIDTH), BF16),
        scratch_shapes=[pltpu.VMEM((A_HEADS, A_HEAD_DIM, A_HEAD_DIM), F32)],
        compiler_params=_params(("arbitrary", "arbitrary")),
        name="delta_rule",
    )(q, k, v, gate, bg, gct, onorm_w.reshape(1, A_HEAD_DIM).astype(F32))


def _ffn_kernel(og_ref, x_ref, wo_ref, fw_ref, wg_ref, wu_ref, wd_ref, aw_ref, kw_ref, wq_ref,
                wkv_ref, h_ref, q_ref, kv_ref, *, d_ff):
    h1 = x_ref[...] + jnp.dot(og_ref[...], wo_ref[...], preferred_element_type=F32)
    hn = _rms(h1, fw_ref[...]).astype(BF16)
    acc = jnp.zeros_like(h1)
    for c in range(d_ff // FFN_COLS):
        cs = slice(c * FFN_COLS, (c + 1) * FFN_COLS)
        g = jnp.dot(hn, wg_ref[:, cs], preferred_element_type=F32)
        u = jnp.dot(hn, wu_ref[:, cs], preferred_element_type=F32)
        acc = acc + jnp.dot((_silu(g) * u).astype(BF16), wd_ref[cs, :],
                            preferred_element_type=F32)
    h2 = h1 + acc
    h_ref[...] = h2
    xh = h2 * lax.rsqrt(jnp.mean(h2 * h2, axis=-1, keepdims=True) + EPS)
    q = jnp.dot((xh * aw_ref[...]).astype(BF16), wq_ref[...], preferred_element_type=F32)
    q_ref[...] = (q * (B_HEAD_DIM ** -0.5)).astype(BF16)
    kv_ref[...] = jnp.dot((xh * kw_ref[...]).astype(BF16), wkv_ref[...],
                          preferred_element_type=F32).astype(BF16)


def _stage_ffn(og, x, w_out, ffn_norm_w, w_gate, w_up, w_down, attn_norm_w, kv_norm_w, w_q, w_kv):
    t, d = x.shape
    d_ff = w_gate.shape[1]
    tm = _tile(t, ROW_TILE)
    hq = B_Q_HEADS * B_HEAD_DIM
    half = B_KV_HEADS * B_HEAD_DIM
    dup = lambda w: jnp.concatenate([w.reshape(d, B_KV_HEADS, 1, B_HEAD_DIM)] * 2, axis=2).reshape(d, 2 * half)
    w_kvd = jnp.concatenate([dup(w_kv[:, :half]), dup(w_kv[:, half:])], axis=1).astype(BF16)
    row_spec = lambda w: pl.BlockSpec((tm, w), lambda i: (i, 0))
    return pl.pallas_call(
        functools.partial(_ffn_kernel, d_ff=d_ff),
        grid=(t // tm,),
        in_specs=[row_spec(A_WIDTH), row_spec(d), _const_spec((A_WIDTH, d)), _const_spec((1, d)),
                  _const_spec((d, d_ff)), _const_spec((d, d_ff)), _const_spec((d_ff, d)),
                  _const_spec((1, d)), _const_spec((1, d)), _const_spec((d, hq)),
                  _const_spec((d, 4 * half))],
        out_specs=[row_spec(d), row_spec(hq), row_spec(4 * half)],
        out_shape=[jax.ShapeDtypeStruct((t, d), F32), jax.ShapeDtypeStruct((t, hq), BF16),
                   jax.ShapeDtypeStruct((t, 4 * half), BF16)],
        compiler_params=_params(("arbitrary",)),
        name="ffn_dense",
    )(og, x, w_out.astype(BF16), ffn_norm_w.reshape(1, d), w_gate.astype(BF16), w_up.astype(BF16),
      w_down.astype(BF16), attn_norm_w.reshape(1, d), kv_norm_w.reshape(1, d), w_q.astype(BF16), w_kvd)


def _rel_buckets(rel):
    nb = REL_BUCKETS // 2
    max_exact = nb // 2
    ret = jnp.where(rel > 0, nb, 0)
    dist = jnp.abs(rel)
    dist_f = jnp.maximum(dist, 1).astype(F32)
    large = max_exact + (jnp.log(dist_f / max_exact) / math.log(REL_MAX_DIST / max_exact)
                         * (nb - max_exact)).astype(I32)
    large = jnp.minimum(large, nb - 1)
    return ret + jnp.where(dist < max_exact, dist, large)


def _bias_kernel(rb_ref, bk_ref, o_ref):
    bk = bk_ref[...]
    for h in range(B_Q_HEADS):
        acc = jnp.zeros(bk.shape, F32)
        for b in range(REL_BUCKETS):
            acc = jnp.where(bk == b, rb_ref[b * B_Q_HEADS + h], acc)
        o_ref[h] = acc


def _stage_bias(rel_bias):
    rel = (jnp.arange(BAND)[None, :] - WINDOW_CHUNKS * CHUNK) - jnp.arange(CHUNK)[:, None]
    buckets = _rel_buckets(rel).astype(I32)
    bias = pl.pallas_call(
        _bias_kernel,
        in_specs=[pl.BlockSpec(memory_space=pltpu.SMEM), pl.BlockSpec(memory_space=pltpu.VMEM)],
        out_specs=pl.BlockSpec(memory_space=pltpu.VMEM),
        out_shape=jax.ShapeDtypeStruct((B_Q_HEADS, CHUNK, BAND), F32),
        name="rel_bias",
    )(rel_bias.astype(F32).reshape(-1), buckets)
    return bias.reshape(B_KV_HEADS, B_GROUP * CHUNK, BAND)


def _attn_kernel(q_ref, kv_ref, kvp_ref, bias_ref, sink_ref, wo_ref, h_ref, o_ref, att_ref, *, nchunk):
    first = pl.program_id(1) == 0
    lane = lax.broadcasted_iota(I32, (CHUNK, LANES), 1)
    lo_half = lane < B_HEAD_DIM
    kidx = lax.broadcasted_iota(I32, (B_GROUP * CHUNK, BAND), 1)
    koff = B_KV_HEADS * LANES
    for j in range(nchunk):
        rows = slice(j * CHUNK, (j + 1) * CHUNK)
        if j >= WINDOW_CHUNKS:
            keys = kv_ref[0, (j - WINDOW_CHUNKS) * CHUNK:(j + 1) * CHUNK, :]
        else:
            keys = jnp.concatenate([kvp_ref[0, j * CHUNK:, :], kv_ref[0, :(j + 1) * CHUNK, :]], axis=0)
        for hk in range(B_KV_HEADS):
            kd = keys[:, hk * LANES:(hk + 1) * LANES]
            vd = keys[:, koff + hk * LANES:koff + (hk + 1) * LANES]
            qa = q_ref[0, rows, (2 * hk) * LANES:(2 * hk + 1) * LANES]
            qb = q_ref[0, rows, (2 * hk + 1) * LANES:(2 * hk + 2) * LANES]
            zero = jnp.zeros_like(qa)
            q4 = jnp.concatenate([jnp.where(lo_half, qa, zero), jnp.where(lo_half, zero, qa),
                                  jnp.where(lo_half, qb, zero), jnp.where(lo_half, zero, qb)], axis=0)
            s = lax.dot_general(q4, kd, (((1,), (1,)), ((), ())), preferred_element_type=F32)
            s = s + bias_ref[hk]
            if j < WINDOW_CHUNKS:
                dead = jnp.logical_and(first, kidx < (WINDOW_CHUNKS - j) * CHUNK)
                s = jnp.where(dead, -jnp.inf, s)
            sink = sink_ref[hk]
            m = jnp.maximum(jnp.max(s, axis=-1, keepdims=True), sink)
            p = jnp.exp(s - m)
            denom = jnp.sum(p, axis=-1, keepdims=True) + jnp.exp(sink - m)
            o = jnp.dot(p.astype(BF16), vd, preferred_element_type=F32) / denom
            ta = jnp.where(lo_half, o[0:CHUNK], o[CHUNK:2 * CHUNK])
            tb = jnp.where(lo_half, o[2 * CHUNK:3 * CHUNK], o[3 * CHUNK:4 * CHUNK])
            att_ref[rows, (2 * hk) * LANES:(2 * hk + 1) * LANES] = ta.astype(BF16)
            att_ref[rows, (2 * hk + 1) * LANES:(2 * hk + 2) * LANES] = tb.astype(BF16)
    o_ref[0] = h_ref[0] + jnp.dot(att_ref[...], wo_ref[...], preferred_element_type=F32)


def _stage_attn(q, kv, bias4, sinks, w_o, h):
    bsz, L, d = h.shape
    tq = _tile(L, ROW_TILE)
    halo = WINDOW_CHUNKS * CHUNK
    per = tq // halo
    hq = B_Q_HEADS * B_HEAD_DIM
    sink4 = jnp.repeat(sinks.astype(F32).reshape(B_KV_HEADS, B_GROUP), CHUNK, axis=1)[..., None]
    row_spec = lambda w: pl.BlockSpec((1, tq, w), lambda b, i: (b, i, 0))
    return pl.pallas_call(
        functools.partial(_attn_kernel, nchunk=tq // CHUNK),
        grid=(bsz, L // tq),
        in_specs=[row_spec(hq), row_spec(kv.shape[-1]),
                  pl.BlockSpec((1, halo, kv.shape[-1]), lambda b, i: (b, jnp.maximum(i * per - 1, 0), 0)),
                  _const_spec(bias4.shape), _const_spec(sink4.shape), _const_spec((hq, d)),
                  row_spec(d)],
        out_specs=row_spec(d),
        out_shape=jax.ShapeDtypeStruct((bsz, L, d), F32),
        scratch_shapes=[pltpu.VMEM((tq, hq), BF16)],
        compiler_params=_params(("arbitrary", "arbitrary")),
        name="swa_attn",
    )(q.reshape(bsz, L, hq), kv.reshape(bsz, L, -1), kv.reshape(bsz, L, -1), bias4, sink4,
      w_o.astype(BF16), h)


def _router_kernel(h_ref, nw_ref, rw_ref, tri_ref, hn_ref, ei_ref, wcol_ref, cnt_ref, carry_ref, *, tr):
    @pl.when(pl.program_id(0) == 0)
    def _():
        carry_ref[...] = jnp.zeros_like(carry_ref)

    hn = _rms(h_ref[...], nw_ref[...])
    hn_ref[...] = hn
    rows = 2 * N_EXPERTS
    lt = lax.dot_general(rw_ref[...], hn.astype(BF16), (((1,), (1,)), ((), ())),
                         preferred_element_type=F32)
    sub = lax.broadcasted_iota(I32, (rows, tr), 0)
    lt = jnp.where(sub < N_EXPERTS, lt, -jnp.inf)
    m1 = jnp.max(lt, axis=0, keepdims=True)
    i1 = jnp.min(jnp.where(lt == m1, sub, rows), axis=0, keepdims=True)
    lt2 = jnp.where(sub == i1, -jnp.inf, lt)
    m2 = jnp.max(lt2, axis=0, keepdims=True)
    i2 = jnp.min(jnp.where(lt2 == m2, sub, rows), axis=0, keepdims=True)
    e2 = jnp.exp(m2 - m1)
    w1 = 1.0 / (1.0 + e2)
    w2 = e2 / (1.0 + e2)
    hit = jnp.logical_or(sub == i1, sub == i2).astype(F32)
    pref = jnp.dot(hit.astype(BF16), tri_ref[...], preferred_element_type=F32) + carry_ref[:, 0:1]
    r1 = jnp.sum(jnp.where(sub == i1, pref, 0.0), axis=0, keepdims=True)
    r2 = jnp.sum(jnp.where(sub == i2, pref, 0.0), axis=0, keepdims=True)
    total = carry_ref[...] + jnp.sum(hit, axis=1, keepdims=True)
    carry_ref[...] = total
    cnt_ref[...] = total
    ei = jnp.where(sub == 0, i1, jnp.where(sub == 1, i2, jnp.where(
        sub == 2, r1.astype(I32), jnp.where(sub == 3, r2.astype(I32), 0))))
    ei_ref[...] = ei[0:8]
    wrow = jnp.where(sub == 0, w1, jnp.where(sub == 1, w2, 0.0))
    wpad = jnp.concatenate([wrow, jnp.zeros((LANES - rows, tr), F32)], axis=0)
    wcol_ref[...] = wpad.T


def _stage_router(h, norm_w, router_w):
    t, d = h.shape
    tr = _tile(t, ROW_TILE)
    rows = 2 * N_EXPERTS
    rwt = jnp.pad(router_w.T, ((0, rows - N_EXPERTS), (0, 0))).astype(BF16)
    r = np.arange(tr)
    tri = jnp.asarray((r[:, None] < r[None, :]).astype(np.float32)).astype(BF16)
    return pl.pallas_call(
        functools.partial(_router_kernel, tr=tr),
        grid=(t // tr,),
        in_specs=[pl.BlockSpec((tr, d), lambda i: (i, 0)), _const_spec((1, d)),
                  _const_spec((rows, d)), _const_spec((tr, tr))],
        out_specs=[pl.BlockSpec((tr, d), lambda i: (i, 0)), pl.BlockSpec((8, tr), lambda i: (0, i)),
                   pl.BlockSpec((tr, LANES), lambda i: (i, 0)),
                   pl.BlockSpec((rows, LANES), lambda i: (0, 0))],
        out_shape=[jax.ShapeDtypeStruct((t, d), F32), jax.ShapeDtypeStruct((8, t), I32),
                   jax.ShapeDtypeStruct((t, LANES), F32), jax.ShapeDtypeStruct((rows, LANES), F32)],
        scratch_shapes=[pltpu.VMEM((rows, LANES), F32)],
        compiler_params=_params(("arbitrary",)),
        name="router",
    )(h, norm_w.reshape(1, d), rwt, tri)


def _row_copy(src_ref, src_row, dst_ref, dst_row, sem):
    return pltpu.make_async_copy(src_ref.at[pl.ds(src_row, 1)], dst_ref.at[pl.ds(dst_row, 1)], sem)


def _dispatch_kernel(d0_ref, d1_ref, hn_ref, xs_in_ref, xs_ref, sem, *, td):
    del xs_in_ref
    base = pl.program_id(0) * td

    def issue(t, carry):
        _row_copy(hn_ref, base + t, xs_ref, d0_ref[base + t], sem).start()
        _row_copy(hn_ref, base + t, xs_ref, d1_ref[base + t], sem).start()
        return carry

    def drain(t, carry):
        _row_copy(hn_ref, 0, xs_ref, 0, sem).wait()
        _row_copy(hn_ref, 0, xs_ref, 0, sem).wait()
        return carry

    lax.fori_loop(0, td, issue, 0)
    lax.fori_loop(0, td, drain, 0)


def _stage_dispatch(dest0, dest1, hn, n_slots):
    t, d = hn.shape
    td = _tile(t, DMA_ROWS)
    xs0 = jnp.zeros((n_slots, d), hn.dtype)
    return pl.pallas_call(
        functools.partial(_dispatch_kernel, td=td),
        grid_spec=pltpu.PrefetchScalarGridSpec(
            num_scalar_prefetch=2, grid=(t // td,),
            in_specs=[pl.BlockSpec(memory_space=pl.ANY), pl.BlockSpec(memory_space=pl.ANY)],
            out_specs=pl.BlockSpec(memory_space=pl.ANY),
            scratch_shapes=[pltpu.SemaphoreType.DMA]),
        out_shape=jax.ShapeDtypeStruct((n_slots, d), hn.dtype),
        input_output_aliases={3: 0},
        compiler_params=pltpu.CompilerParams(dimension_semantics=("arbitrary",), has_side_effects=True),
        name="moe_dispatch",
    )(dest0, dest1, hn, xs0)


def _moe_kernel(be_ref, nu_ref, xs_ref, wg_ref, wu_ref, wd_ref, ys_ref, xb_ref):
    del be_ref
    f = pl.program_id(1)

    @pl.when(pl.program_id(0) < nu_ref[0])
    def _():
        @pl.when(f == 0)
        def _():
            xb_ref[...] = xs_ref[...].astype(BF16)

        x = xb_ref[...]
        g = jnp.dot(x, wg_ref[0], preferred_element_type=F32)
        u = jnp.dot(x, wu_ref[0], preferred_element_type=F32)
        y = jnp.dot((_silu(g) * u).astype(BF16), wd_ref[0], preferred_element_type=F32)

        @pl.when(f == 0)
        def _():
            ys_ref[...] = y

        @pl.when(f > 0)
        def _():
            ys_ref[...] += y

    @pl.when(jnp.logical_and(pl.program_id(0) >= nu_ref[0], f == 0))
    def _():
        ys_ref[...] = jnp.zeros_like(ys_ref)


def _stage_moe(blk_e, n_used, xs, w_gate, w_up, w_down):
    n_slots, d = xs.shape
    d_ff = w_gate.shape[-1]
    tm = MOE_ROWS
    tf = _tile(d_ff, MOE_COLS)
    nb, nf = n_slots // tm, d_ff // tf

    def blk(b, nu):
        return jnp.minimum(b, nu[0] - 1)

    def col(b, f, nu):
        return jnp.where(b < nu[0], f, nf - 1)

    return pl.pallas_call(
        _moe_kernel,
        grid_spec=pltpu.PrefetchScalarGridSpec(
            num_scalar_prefetch=2, grid=(nb, nf),
            in_specs=[pl.BlockSpec((tm, d), lambda b, f, be, nu: (blk(b, nu), 0)),
                      pl.BlockSpec((1, d, tf), lambda b, f, be, nu: (be[blk(b, nu)], 0, col(b, f, nu))),
                      pl.BlockSpec((1, d, tf), lambda b, f, be, nu: (be[blk(b, nu)], 0, col(b, f, nu))),
                      pl.BlockSpec((1, tf, d), lambda b, f, be, nu: (be[blk(b, nu)], col(b, f, nu), 0))],
            out_specs=pl.BlockSpec((tm, d), lambda b, f, be, nu: (b, 0)),
            scratch_shapes=[pltpu.VMEM((tm, d), BF16)]),
        out_shape=jax.ShapeDtypeStruct((n_slots, d), F32),
        compiler_params=_params(("arbitrary", "arbitrary")),
        name="moe_experts",
    )(blk_e, n_used, xs, w_gate.astype(BF16), w_up.astype(BF16), w_down.astype(BF16))


def _combine_kernel(d0_ref, d1_ref, h_ref, w_ref, nw_ref, ys_ref, o_ref, y0_ref, y1_ref, sem, *, tc):
    base = pl.program_id(0) * tc

    def issue(t, carry):
        _row_copy(ys_ref, d0_ref[base + t], y0_ref, t, sem).start()
        _row_copy(ys_ref, d1_ref[base + t], y1_ref, t, sem).start()
        return carry

    def drain(t, carry):
        _row_copy(ys_ref, 0, y0_ref, 0, sem).wait()
        _row_copy(ys_ref, 0, y1_ref, 0, sem).wait()
        return carry

    lax.fori_loop(0, tc, issue, 0)
    lax.fori_loop(0, tc, drain, 0)
    h = h_ref[...] + (y0_ref[...] * w_ref[:, 0:1] + y1_ref[...] * w_ref[:, 1:2])
    o_ref[...] = _rms(h, nw_ref[...])


def _stage_combine(dest0, dest1, h, wcol, norm_w, ys):
    t, d = h.shape
    tc = _tile(t, DMA_ROWS)
    return pl.pallas_call(
        functools.partial(_combine_kernel, tc=tc),
        grid_spec=pltpu.PrefetchScalarGridSpec(
            num_scalar_prefetch=2, grid=(t // tc,),
            in_specs=[pl.BlockSpec((tc, d), lambda i, *_: (i, 0)),
                      pl.BlockSpec((tc, LANES), lambda i, *_: (i, 0)),
                      pl.BlockSpec((1, d), lambda i, *_: (0, 0)),
                      pl.BlockSpec(memory_space=pl.ANY)],
            out_specs=pl.BlockSpec((tc, d), lambda i, *_: (i, 0)),
            scratch_shapes=[pltpu.VMEM((tc, d), F32), pltpu.VMEM((tc, d), F32),
                            pltpu.SemaphoreType.DMA]),
        out_shape=jax.ShapeDtypeStruct((t, d), F32),
        compiler_params=_params(("arbitrary",)),
        name="moe_combine",
    )(dest0, dest1, h, wcol, norm_w.reshape(1, d), ys)


def kernel(x, attn_norm_w, ffn_norm_w, a_w_in, a_conv_w, a_a_log, a_dt_bias, a_onorm_w, a_w_out,
           kv_norm_w, w_kv, b_w_q, b_sinks, b_w_o, rel_bias, ffn_w_gate, ffn_w_up, ffn_w_down,
           moe_router, moe_w_gate, moe_w_up, moe_w_down, final_norm_w):
    bsz, L, d = x.shape
    t = bsz * L

    q, k, v, gate, bg, gct = _stage_in(x, attn_norm_w[0], a_w_in[0], a_conv_w[0], a_a_log[0],
                                       a_dt_bias[0])
    og = _stage_delta(q, k, v, gate, bg, gct, a_onorm_w[0])
    h, q2, kv2 = _stage_ffn(og.reshape(t, A_WIDTH), x.reshape(t, d), a_w_out[0], ffn_norm_w[0],
                            ffn_w_gate[0], ffn_w_up[0], ffn_w_down[0], attn_norm_w[1], kv_norm_w,
                            b_w_q[0], w_kv)

    bias4 = _stage_bias(rel_bias)
    h = _stage_attn(q2, kv2, bias4, b_sinks[0], b_w_o[0], h.reshape(bsz, L, d)).reshape(t, d)

    hn, ei, wcol, cnt = _stage_router(h, ffn_norm_w[1], moe_router[0])
    counts = cnt[:N_EXPERTS, 0].astype(I32)
    p_counts = (counts + MOE_ROWS - 1) // MOE_ROWS * MOE_ROWS
    p_end = jnp.cumsum(p_counts)
    p_start = p_end - p_counts
    dest0 = p_start[ei[0]] + ei[2]
    dest1 = p_start[ei[1]] + ei[3]
    n_blocks = -(-(2 * t) // MOE_ROWS) + N_EXPERTS
    blk_e = jnp.minimum(jnp.searchsorted(p_end, jnp.arange(n_blocks, dtype=I32) * MOE_ROWS, side='right'),
                        N_EXPERTS - 1).astype(I32)
    n_used = (p_end[-1:] // MOE_ROWS).astype(I32)
    xs = _stage_dispatch(dest0, dest1, hn, n_blocks * MOE_ROWS)
    ys = _stage_moe(blk_e, n_used, xs, moe_w_gate[0], moe_w_up[0], moe_w_down[0])
    out = _stage_combine(dest0, dest1, h, wcol, final_norm_w, ys)
    return out.reshape(bsz, L, d)
```

```python
import functools
import math

import numpy as np
import jax
import jax.numpy as jnp
from jax import lax
from jax.experimental import pallas as pl
from jax.experimental.pallas import tpu as pltpu

F32 = jnp.float32
BF16 = jnp.bfloat16
I32 = jnp.int32

EPS = 1e-6
CHUNK = 64
A_HEADS = 8
A_HEAD_DIM = 128
A_WIDTH = A_HEADS * A_HEAD_DIM
A_CONV = 4
B_Q_HEADS = 16
B_KV_HEADS = 4
B_HEAD_DIM = 64
B_GROUP = B_Q_HEADS // B_KV_HEADS
WINDOW_CHUNKS = 2
BAND = (WINDOW_CHUNKS + 1) * CHUNK
REL_BUCKETS = 32
REL_MAX_DIST = 128
N_EXPERTS = 8
LANES = 128

VMEM_LIMIT = 56 * 1024 * 1024

ROW_TILE = 512
CONV_COLS = 512
FFN_COLS = 256
MOE_ROWS = 1024
MOE_COLS = 512
LOCAL_CHUNKS = 2
DMA_ROWS = 512


def _tile(n, pref):
    t = min(n, pref)
    assert n % t == 0, (n, t)
    return t


def _mm(a, b):
    return jnp.dot(a.astype(BF16), b.astype(BF16), preferred_element_type=F32)


def _mm_nt(a, b):
    return lax.dot_general(a.astype(BF16), b.astype(BF16), (((1,), (1,)), ((), ())),
                           preferred_element_type=F32)


def _mm_tn(a, b):
    return lax.dot_general(a.astype(BF16), b.astype(BF16), (((0,), (0,)), ((), ())),
                           preferred_element_type=F32)


def _silu(x):
    return x * jax.nn.sigmoid(x)


def _rms(x, w):
    return x * lax.rsqrt(jnp.mean(x * x, axis=-1, keepdims=True) + EPS) * w


def _const_spec(shape):
    nd = len(shape)
    return pl.BlockSpec(shape, lambda *_: (0,) * nd, pipeline_mode=pl.Buffered(1))


def _params(sem):
    return pltpu.CompilerParams(dimension_semantics=sem, vmem_limit_bytes=VMEM_LIMIT)


def _in_kernel(x_ref, nw_ref, w_ref, wba_ref, cw_ref, ap_ref, seg_ref,
               q_ref, k_ref, v_ref, gate_ref, bg_ref, gct_ref, ext_ref, *, tm):
    @pl.when(pl.program_id(1) == 0)
    def _():
        ext_ref[0:8, :] = jnp.zeros((8, 3 * A_WIDTH), F32)

    hn = _rms(x_ref[0], nw_ref[...]).astype(BF16)
    outs = (q_ref, k_ref, v_ref)
    for c in range(3 * A_WIDTH // CONV_COLS):
        cs = slice(c * CONV_COLS, (c + 1) * CONV_COLS)
        p = jnp.dot(hn, w_ref[:, cs], preferred_element_type=F32)
        ext_ref[8:tm + 8, cs] = p
        acc = cw_ref[A_CONV - 1:A_CONV, cs] * p
        for j in range(A_CONV - 1):
            acc = acc + cw_ref[j:j + 1, cs] * ext_ref[5 + j:5 + j + tm, cs]
        ext_ref[0:8, cs] = p[tm - 8:, :]
        a = _silu(acc)
        which, off = divmod(c * CONV_COLS, A_WIDTH)
        for hh in range(CONV_COLS // A_HEAD_DIM):
            hs = a[:, hh * A_HEAD_DIM:(hh + 1) * A_HEAD_DIM]
            if which < 2:
                hs = hs * lax.rsqrt(jnp.sum(hs * hs, axis=-1, keepdims=True) + EPS)
            if which == 0:
                hs = hs * (A_HEAD_DIM ** -0.5)
            lo = off + hh * A_HEAD_DIM
            outs[which][0, :, lo:lo + A_HEAD_DIM] = hs.astype(BF16)

    for c in range(A_WIDTH // CONV_COLS):
        cs = slice(3 * A_WIDTH + c * CONV_COLS, 3 * A_WIDTH + (c + 1) * CONV_COLS)
        gate_ref[0, :, c * CONV_COLS:(c + 1) * CONV_COLS] = jnp.dot(
            hn, w_ref[:, cs], preferred_element_type=F32)

    ba = jnp.dot(hn, wba_ref[...], preferred_element_type=F32)
    beta = jax.nn.sigmoid(ba)
    z = ba + ap_ref[1:2, :]
    softplus = jnp.maximum(z, 0.0) + jnp.log1p(jnp.exp(-jnp.abs(z)))
    g = -jnp.exp(ap_ref[0:1, :]) * softplus
    g1 = g.astype(BF16)
    r1 = g - g1.astype(F32)
    g2 = r1.astype(BF16)
    g3 = (r1 - g2.astype(F32)).astype(BF16)
    seg = seg_ref[...]
    gc = (jnp.dot(seg, g1, preferred_element_type=F32) + jnp.dot(seg, g2, preferred_element_type=F32)
          + jnp.dot(seg, g3, preferred_element_type=F32))
    lane = lax.broadcasted_iota(I32, (tm, LANES), 1)
    bg = jnp.where(lane < A_HEADS, beta, gc)
    bg_ref[0] = bg
    bgt = bg.T
    for ci in range(tm // CHUNK):
        gct_ref[0, ci] = bgt[0:2 * A_HEADS, ci * CHUNK:(ci + 1) * CHUNK]


def _stage_in(x, norm_w, w_in, conv_w, a_log, dt_bias):
    bsz, L, d = x.shape
    tm = _tile(L, ROW_TILE)
    nc = L // CHUNK
    w_main = w_in[:, :4 * A_WIDTH].astype(BF16)
    w_ba = jnp.pad(w_in[:, 4 * A_WIDTH:], ((0, 0), (0, LANES - 2 * A_HEADS))).astype(BF16)
    ap = jnp.zeros((2, LANES), F32)
    ap = ap.at[0, A_HEADS:2 * A_HEADS].set(a_log.astype(F32))
    ap = ap.at[1, A_HEADS:2 * A_HEADS].set(dt_bias.astype(F32))
    r = np.arange(tm)
    seg = jnp.asarray(((r[:, None] >= r[None, :]) &
                       (r[:, None] // CHUNK == r[None, :] // CHUNK)).astype(np.float32)).astype(BF16)
    row_spec = lambda w: pl.BlockSpec((1, tm, w), lambda b, i: (b, i, 0))
    act = jax.ShapeDtypeStruct((bsz, L, A_WIDTH), BF16)
    return pl.pallas_call(
        functools.partial(_in_kernel, tm=tm),
        grid=(bsz, L // tm),
        in_specs=[row_spec(d), _const_spec((1, d)), _const_spec((d, 4 * A_WIDTH)),
                  _const_spec((d, LANES)), _const_spec((A_CONV, 3 * A_WIDTH)),
                  _const_spec((2, LANES)), _const_spec((tm, tm))],
        out_specs=[row_spec(A_WIDTH), row_spec(A_WIDTH), row_spec(A_WIDTH), row_spec(A_WIDTH),
                   row_spec(LANES),
                   pl.BlockSpec((1, tm // CHUNK, 2 * A_HEADS, CHUNK), lambda b, i: (b, i, 0, 0))],
        out_shape=[act, act, act, jax.ShapeDtypeStruct((bsz, L, A_WIDTH), F32),
                   jax.ShapeDtypeStruct((bsz, L, LANES), F32),
                   jax.ShapeDtypeStruct((bsz, nc, 2 * A_HEADS, CHUNK), F32)],
        scratch_shapes=[pltpu.VMEM((tm + 8, 3 * A_WIDTH), F32)],
        compiler_params=_params(("arbitrary", "arbitrary")),
        name="in_proj",
    )(x, norm_w.reshape(1, d), w_main, w_ba, conv_w.astype(F32), ap, seg)


def _delta_kernel(q_ref, k_ref, v_ref, gate_ref, bg_ref, gct_ref, ow_ref, o_ref,
                  s_ref, u_ref, w_ref, qd_ref, kd_ref, at_ref, *, cb):
    @pl.when(pl.program_id(1) == 0)
    def _():
        s_ref[...] = jnp.zeros_like(s_ref)

    heads = range(A_HEADS)
    hcols = [slice(h * A_HEAD_DIM, (h + 1) * A_HEAD_DIM) for h in heads]
    row = lax.broadcasted_iota(I32, (CHUNK, CHUNK), 0)
    col = lax.broadcasted_iota(I32, (CHUNK, CHUNK), 1)
    tri = row >= col
    strict = row > col
    diag = (row >> 4) == (col >> 4)
    eye = (row == col).astype(F32)

    def local(cc, carry):
        items = [(j, h) for j in range(LOCAL_CHUNKS) for h in heads]
        its = range(len(items))
        cidx = [cc * LOCAL_CHUNKS + j for j in range(LOCAL_CHUNKS)]
        crow = [pl.ds(pl.multiple_of(c * CHUNK, CHUNK), CHUNK) for c in cidx]
        rows = [crow[j] for j, _ in items]
        cols = [hcols[h] for _, h in items]
        qb = [q_ref[0, rows[i], cols[i]] for i in its]
        kb = [k_ref[0, rows[i], cols[i]] for i in its]
        vb = [v_ref[0, rows[i], cols[i]] for i in its]
        bgs = [bg_ref[0, crow[j], :] for j in range(LOCAL_CHUNKS)]
        gts = [gct_ref[0, c] for c in cidx]
        beta = [bgs[j][:, h:h + 1] for j, h in items]
        gcol = [bgs[j][:, A_HEADS + h:A_HEADS + h + 1] for j, h in items]
        grow = [gts[j][A_HEADS + h:A_HEADS + h + 1, :] for j, h in items]

        decay = [jnp.exp(jnp.where(tri, gcol[i] - grow[i], -jnp.inf)) for i in its]
        eg = [jnp.exp(gcol[i]) for i in its]
        kf = [kb[i].astype(F32) for i in its]
        kbeta = [kf[i] * beta[i] for i in its]
        lower = [jnp.where(strict, _mm_nt(kbeta[i], kb[i]) * decay[i], 0.0) for i in its]
        ld = [jnp.where(diag, lower[i], 0.0) for i in its]
        lo = [lower[i] - ld[i] for i in its]
        p = [eye - ld[i] for i in its]
        s = [_mm(ld[i], ld[i]) for i in its]
        for _ in range(2):
            p = [p[i] + _mm(p[i], s[i]) for i in its]
            s = [_mm(s[i], s[i]) for i in its]
        td = [p[i] + _mm(p[i], s[i]) for i in its]
        rhs = [jnp.concatenate([vb[i].astype(F32) * beta[i], kbeta[i] * eg[i]], axis=1) for i in its]
        r = [_mm(td[i], rhs[i]) for i in its]
        m = [_mm(td[i], lo[i]) for i in its]
        m2 = [_mm(m[i], m[i]) for i in its]
        r = [r[i] + _mm(m2[i], r[i]) for i in its]
        uw = [r[i] - _mm(m[i], r[i]) for i in its]
        attn = [_mm_nt(qb[i], kb[i]) * decay[i] for i in its]
        qd = [qb[i].astype(F32) * eg[i] for i in its]
        kd = [kf[i] * jnp.exp(grow[i][:, CHUNK - 1:CHUNK] - gcol[i]) for i in its]

        for i, (_, h) in enumerate(items):
            u_ref[rows[i], cols[i]] = uw[i][:, :A_HEAD_DIM]
            w_ref[rows[i], cols[i]] = uw[i][:, A_HEAD_DIM:].astype(BF16)
            qd_ref[rows[i], cols[i]] = qd[i].astype(BF16)
            kd_ref[rows[i], cols[i]] = kd[i].astype(BF16)
            at_ref[h, rows[i], :] = attn[i].astype(BF16)
        return carry

    def recur(c, carry):
        rows = pl.ds(pl.multiple_of(c * CHUNK, CHUNK), CHUNK)
        wq = [jnp.concatenate([w_ref[rows, hcols[h]], qd_ref[rows, hcols[h]]], axis=0) for h in heads]
        u = [u_ref[rows, hcols[h]] for h in heads]
        kd = [kd_ref[rows, hcols[h]] for h in heads]
        at = [at_ref[h, rows, :] for h in heads]
        st = [s_ref[h] for h in heads]
        gate = [gate_ref[0, rows, hcols[h]] for h in heads]
        gts = gct_ref[0, c]
        eglast = [jnp.exp(gts[A_HEADS + h:A_HEADS + h + 1, CHUNK - 1:CHUNK]) for h in heads]
        ow = ow_ref[...]

        ws_qs = [_mm(wq[h], st[h]) for h in heads]
        vnew = [u[h] - ws_qs[h][:CHUNK] for h in heads]
        o = [ws_qs[h][CHUNK:] + _mm(at[h], vnew[h]) for h in heads]
        snew = [st[h] * eglast[h] + _mm_tn(kd[h], vnew[h]) for h in heads]
        og = [_rms(o[h], ow) * _silu(gate[h]) for h in heads]

        for h in heads:
            s_ref[h] = snew[h]
            o_ref[0, rows, hcols[h]] = og[h].astype(BF16)
        return carry

    lax.fori_loop(0, cb // LOCAL_CHUNKS, local, 0)
    lax.fori_loop(0, cb, recur, 0)


def _stage_delta(q, k, v, gate, bg, gct, onorm_w):
    bsz, L, _ = q.shape
    tm = _tile(L, ROW_TILE)
    cb = tm // CHUNK
    row_spec = lambda w: pl.BlockSpec((1, tm, w), lambda b, i: (b, i, 0))
    return pl.pallas_call(
        functools.partial(_delta_kernel, cb=cb),
        grid=(bsz, L // tm),
        in_specs=[row_spec(A_WIDTH), row_spec(A_WIDTH), row_spec(A_WIDTH), row_spec(A_WIDTH),
                  row_spec(LANES),
                  pl.BlockSpec((1, cb, 2 * A_HEADS, CHUNK), lambda b, i: (b, i, 0, 0)),
                  _const_spec((1, A_HEAD_DIM))],
        out_specs=row_spec(A_WIDTH),
        out_shape=jax.ShapeDtypeStruct((bsz, L, A_WIDTH), BF16),
        scratch_shapes=[pltpu.VMEM((A_HEADS, A_HEAD_DIM, A_HEAD_DIM), F32),
                        pltpu.VMEM((tm, A_WIDTH), F32), pltpu.VMEM((tm, A_WIDTH), BF16),
                        pltpu.VMEM((tm, A_WIDTH), BF16), pltpu.VMEM((tm, A_WIDTH), BF16),
                        pltpu.VMEM((A_HEADS, tm, CHUNK), BF16)],
        compiler_params=_params(("arbitrary", "arbitrary")),
        name="delta_rule",
    )(q, k, v, gate, bg, gct, onorm_w.reshape(1, A_HEAD_DIM).astype(F32))


def _ffn_kernel(og_ref, x_ref, wo_ref, fw_ref, wg_ref, wu_ref, wd_ref, aw_ref, kw_ref, wq_ref,
                wkv_ref, h_ref, q_ref, kv_ref, *, d_ff):
    h1 = x_ref[...] + jnp.dot(og_ref[...], wo_ref[...], preferred_element_type=F32)
    hn = _rms(h1, fw_ref[...]).astype(BF16)
    acc = jnp.zeros_like(h1)
    for c in range(d_ff // FFN_COLS):
        cs = slice(c * FFN_COLS, (c + 1) * FFN_COLS)
        g = jnp.dot(hn, wg_ref[:, cs], preferred_element_type=F32)
        u = jnp.dot(hn, wu_ref[:, cs], preferred_element_type=F32)
        acc = acc + jnp.dot((_silu(g) * u).astype(BF16), wd_ref[cs, :],
                            preferred_element_type=F32)
    h2 = h1 + acc
    h_ref[...] = h2
    xh = h2 * lax.rsqrt(jnp.mean(h2 * h2, axis=-1, keepdims=True) + EPS)
    q = jnp.dot((xh * aw_ref[...]).astype(BF16), wq_ref[...], preferred_element_type=F32)
    q_ref[...] = (q * (B_HEAD_DIM ** -0.5)).astype(BF16)
    kv_ref[...] = jnp.dot((xh * kw_ref[...]).astype(BF16), wkv_ref[...],
                          preferred_element_type=F32).astype(BF16)


def _stage_ffn(og, x, w_out, ffn_norm_w, w_gate, w_up, w_down, attn_norm_w, kv_norm_w, w_q, w_kv):
    t, d = x.shape
    d_ff = w_gate.shape[1]
    tm = _tile(t, ROW_TILE)
    hq = B_Q_HEADS * B_HEAD_DIM
    half = B_KV_HEADS * B_HEAD_DIM
    dup = lambda w: jnp.concatenate([w.reshape(d, B_KV_HEADS, 1, B_HEAD_DIM)] * 2, axis=2).reshape(d, 2 * half)
    w_kvd = jnp.concatenate([dup(w_kv[:, :half]), dup(w_kv[:, half:])], axis=1).astype(BF16)
    row_spec = lambda w: pl.BlockSpec((tm, w), lambda i: (i, 0))
    return pl.pallas_call(
        functools.partial(_ffn_kernel, d_ff=d_ff),
        grid=(t // tm,),
        in_specs=[row_spec(A_WIDTH), row_spec(d), _const_spec((A_WIDTH, d)), _const_spec((1, d)),
                  _const_spec((d, d_ff)), _const_spec((d, d_ff)), _const_spec((d_ff, d)),
                  _const_spec((1, d)), _const_spec((1, d)), _const_spec((d, hq)),
                  _const_spec((d, 4 * half))],
        out_specs=[row_spec(d), row_spec(hq), row_spec(4 * half)],
        out_shape=[jax.ShapeDtypeStruct((t, d), F32), jax.ShapeDtypeStruct((t, hq), BF16),
                   jax.ShapeDtypeStruct((t, 4 * half), BF16)],
        compiler_params=_params(("arbitrary",)),
        name="ffn_dense",
    )(og, x, w_out.astype(BF16), ffn_norm_w.reshape(1, d), w_gate.astype(BF16), w_up.astype(BF16),
      w_down.astype(BF16), attn_norm_w.reshape(1, d), kv_norm_w.reshape(1, d), w_q.astype(BF16), w_kvd)


def _rel_buckets(rel):
    nb = REL_BUCKETS // 2
    max_exact = nb // 2
    ret = jnp.where(rel > 0, nb, 0)
    dist = jnp.abs(rel)
    dist_f = jnp.maximum(dist, 1).astype(F32)
    large = max_exact + (jnp.log(dist_f / max_exact) / math.log(REL_MAX_DIST / max_exact)
                         * (nb - max_exact)).astype(I32)
    large = jnp.minimum(large, nb - 1)
    return ret + jnp.where(dist < max_exact, dist, large)


def _bias_kernel(rb_ref, bk_ref, o_ref):
    bk = bk_ref[...]
    for h in range(B_Q_HEADS):
        acc = jnp.zeros(bk.shape, F32)
        for b in range(REL_BUCKETS):
            acc = jnp.where(bk == b, rb_ref[b * B_Q_HEADS + h], acc)
        o_ref[h] = acc


def _stage_bias(rel_bias):
    rel = (jnp.arange(BAND)[None, :] - WINDOW_CHUNKS * CHUNK) - jnp.arange(CHUNK)[:, None]
    buckets = _rel_buckets(rel).astype(I32)
    bias = pl.pallas_call(
        _bias_kernel,
        in_specs=[pl.BlockSpec(memory_space=pltpu.SMEM), pl.BlockSpec(memory_space=pltpu.VMEM)],
        out_specs=pl.BlockSpec(memory_space=pltpu.VMEM),
        out_shape=jax.ShapeDtypeStruct((B_Q_HEADS, CHUNK, BAND), F32),
        name="rel_bias",
    )(rel_bias.astype(F32).reshape(-1), buckets)
    return bias.reshape(B_KV_HEADS, B_GROUP * CHUNK, BAND)


def _attn_kernel(q_ref, kv_ref, kvp_ref, bias_ref, sink_ref, wo_ref, h_ref, o_ref, att_ref, *, nchunk):
    first = pl.program_id(1) == 0
    lane = lax.broadcasted_iota(I32, (CHUNK, LANES), 1)
    lo_half = lane < B_HEAD_DIM
    kidx = lax.broadcasted_iota(I32, (B_GROUP * CHUNK, BAND), 1)
    koff = B_KV_HEADS * LANES
    for j in range(nchunk):
        rows = slice(j * CHUNK, (j + 1) * CHUNK)
        if j >= WINDOW_CHUNKS:
            keys = kv_ref[0, (j - WINDOW_CHUNKS) * CHUNK:(j + 1) * CHUNK, :]
        else:
            keys = jnp.concatenate([kvp_ref[0, j * CHUNK:, :], kv_ref[0, :(j + 1) * CHUNK, :]], axis=0)
        for hk in range(B_KV_HEADS):
            kd = keys[:, hk * LANES:(hk + 1) * LANES]
            vd = keys[:, koff + hk * LANES:koff + (hk + 1) * LANES]
            qa = q_ref[0, rows, (2 * hk) * LANES:(2 * hk + 1) * LANES]
            qb = q_ref[0, rows, (2 * hk + 1) * LANES:(2 * hk + 2) * LANES]
            zero = jnp.zeros_like(qa)
            q4 = jnp.concatenate([jnp.where(lo_half, qa, zero), jnp.where(lo_half, zero, qa),
                                  jnp.where(lo_half, qb, zero), jnp.where(lo_half, zero, qb)], axis=0)
            s = lax.dot_general(q4, kd, (((1,), (1,)), ((), ())), preferred_element_type=F32)
            s = s + bias_ref[hk]
            if j < WINDOW_CHUNKS:
                dead = jnp.logical_and(first, kidx < (WINDOW_CHUNKS - j) * CHUNK)
                s = jnp.where(dead, -jnp.inf, s)
            sink = sink_ref[hk]
            m = jnp.maximum(jnp.max(s, axis=-1, keepdims=True), sink)
            p = jnp.exp(s - m)
            denom = jnp.sum(p, axis=-1, keepdims=True) + jnp.exp(sink - m)
            o = jnp.dot(p.astype(BF16), vd, preferred_element_type=F32) / denom
            ta = jnp.where(lo_half, o[0:CHUNK], o[CHUNK:2 * CHUNK])
            tb = jnp.where(lo_half, o[2 * CHUNK:3 * CHUNK], o[3 * CHUNK:4 * CHUNK])
            att_ref[rows, (2 * hk) * LANES:(2 * hk + 1) * LANES] = ta.astype(BF16)
            att_ref[rows, (2 * hk + 1) * LANES:(2 * hk + 2) * LANES] = tb.astype(BF16)
    o_ref[0] = h_ref[0] + jnp.dot(att_ref[...], wo_ref[...], preferred_element_type=F32)


def _stage_attn(q, kv, bias4, sinks, w_o, h):
    bsz, L, d = h.shape
    tq = _tile(L, ROW_TILE)
    halo = WINDOW_CHUNKS * CHUNK
    per = tq // halo
    hq = B_Q_HEADS * B_HEAD_DIM
    sink4 = jnp.repeat(sinks.astype(F32).reshape(B_KV_HEADS, B_GROUP), CHUNK, axis=1)[..., None]
    row_spec = lambda w: pl.BlockSpec((1, tq, w), lambda b, i: (b, i, 0))
    return pl.pallas_call(
        functools.partial(_attn_kernel, nchunk=tq // CHUNK),
        grid=(bsz, L // tq),
        in_specs=[row_spec(hq), row_spec(kv.shape[-1]),
                  pl.BlockSpec((1, halo, kv.shape[-1]), lambda b, i: (b, jnp.maximum(i * per - 1, 0), 0)),
                  _const_spec(bias4.shape), _const_spec(sink4.shape), _const_spec((hq, d)),
                  row_spec(d)],
        out_specs=row_spec(d),
        out_shape=jax.ShapeDtypeStruct((bsz, L, d), F32),
        scratch_shapes=[pltpu.VMEM((tq, hq), BF16)],
        compiler_params=_params(("arbitrary", "arbitrary")),
        name="swa_attn",
    )(q.reshape(bsz, L, hq), kv.reshape(bsz, L, -1), kv.reshape(bsz, L, -1), bias4, sink4,
      w_o.astype(BF16), h)


def _router_kernel(h_ref, nw_ref, rw_ref, tri_ref, hn_ref, ei_ref, wcol_ref, cnt_ref, carry_ref, *, tr):
    @pl.when(pl.program_id(0) == 0)
    def _():
        carry_ref[...] = jnp.zeros_like(carry_ref)

    hn = _rms(h_ref[...], nw_ref[...])
    hn_ref[...] = hn
    rows = 2 * N_EXPERTS
    lt = lax.dot_general(rw_ref[...], hn.astype(BF16), (((1,), (1,)), ((), ())),
                         preferred_element_type=F32)
    sub = lax.broadcasted_iota(I32, (rows, tr), 0)
    lt = jnp.where(sub < N_EXPERTS, lt, -jnp.inf)
    m1 = jnp.max(lt, axis=0, keepdims=True)
    i1 = jnp.min(jnp.where(lt == m1, sub, rows), axis=0, keepdims=True)
    lt2 = jnp.where(sub == i1, -jnp.inf, lt)
    m2 = jnp.max(lt2, axis=0, keepdims=True)
    i2 = jnp.min(jnp.where(lt2 == m2, sub, rows), axis=0, keepdims=True)
    e2 = jnp.exp(m2 - m1)
    w1 = 1.0 / (1.0 + e2)
    w2 = e2 / (1.0 + e2)
    hit = jnp.logical_or(sub == i1, sub == i2).astype(F32)
    pref = jnp.dot(hit.astype(BF16), tri_ref[...], preferred_element_type=F32) + carry_ref[:, 0:1]
    r1 = jnp.sum(jnp.where(sub == i1, pref, 0.0), axis=0, keepdims=True)
    r2 = jnp.sum(jnp.where(sub == i2, pref, 0.0), axis=0, keepdims=True)
    total = carry_ref[...] + jnp.sum(hit, axis=1, keepdims=True)
    carry_ref[...] = total
    cnt_ref[...] = total
    ei = jnp.where(sub == 0, i1, jnp.where(sub == 1, i2, jnp.where(
        sub == 2, r1.astype(I32), jnp.where(sub == 3, r2.astype(I32), 0))))
    ei_ref[...] = ei[0:8]
    wrow = jnp.where(sub == 0, w1, jnp.where(sub == 1, w2, 0.0))
    wpad = jnp.concatenate([wrow, jnp.zeros((LANES - rows, tr), F32)], axis=0)
    wcol_ref[...] = wpad.T


def _stage_router(h, norm_w, router_w):
    t, d = h.shape
    tr = _tile(t, ROW_TILE)
    rows = 2 * N_EXPERTS
    rwt = jnp.pad(router_w.T, ((0, rows - N_EXPERTS), (0, 0))).astype(BF16)
    r = np.arange(tr)
    tri = jnp.asarray((r[:, None] < r[None, :]).astype(np.float32)).astype(BF16)
    return pl.pallas_call(
        functools.partial(_router_kernel, tr=tr),
        grid=(t // tr,),
        in_specs=[pl.BlockSpec((tr, d), lambda i: (i, 0)), _const_spec((1, d)),
                  _const_spec((rows, d)), _const_spec((tr, tr))],
        out_specs=[pl.BlockSpec((tr, d), lambda i: (i, 0)), pl.BlockSpec((8, tr), lambda i: (0, i)),
                   pl.BlockSpec((tr, LANES), lambda i: (i, 0)),
                   pl.BlockSpec((rows, LANES), lambda i: (0, 0))],
        out_shape=[jax.ShapeDtypeStruct((t, d), F32), jax.ShapeDtypeStruct((8, t), I32),
                   jax.ShapeDtypeStruct((t, LANES), F32), jax.ShapeDtypeStruct((rows, LANES), F32)],
        scratch_shapes=[pltpu.VMEM((rows, LANES), F32)],
        compiler_params=_params(("arbitrary",)),
        name="router",
    )(h, norm_w.reshape(1, d), rwt, tri)


def _row_copy(src_ref, src_row, dst_ref, dst_row, sem):
    return pltpu.make_async_copy(src_ref.at[pl.ds(src_row, 1)], dst_ref.at[pl.ds(dst_row, 1)], sem)


def _dispatch_kernel(d0_ref, d1_ref, hn_ref, xs_in_ref, xs_ref, sem, *, td):
    del xs_in_ref
    base = pl.program_id(0) * td

    def issue(t, carry):
        _row_copy(hn_ref, t, xs_ref, d0_ref[base + t], sem).start()
        _row_copy(hn_ref, t, xs_ref, d1_ref[base + t], sem).start()
        return carry

    def drain(t, carry):
        _row_copy(hn_ref, 0, xs_ref, 0, sem).wait()
        _row_copy(hn_ref, 0, xs_ref, 0, sem).wait()
        return carry

    lax.fori_loop(0, td, issue, 0, unroll=8)
    lax.fori_loop(0, td, drain, 0, unroll=8)


def _stage_dispatch(dest0, dest1, hn, n_slots):
    t, d = hn.shape
    td = _tile(t, DMA_ROWS)
    xs0 = jnp.zeros((n_slots, d), hn.dtype)
    return pl.pallas_call(
        functools.partial(_dispatch_kernel, td=td),
        grid_spec=pltpu.PrefetchScalarGridSpec(
            num_scalar_prefetch=2, grid=(t // td,),
            in_specs=[pl.BlockSpec((td, d), lambda i, *_: (i, 0)), pl.BlockSpec(memory_space=pl.ANY)],
            out_specs=pl.BlockSpec(memory_space=pl.ANY),
            scratch_shapes=[pltpu.SemaphoreType.DMA]),
        out_shape=jax.ShapeDtypeStruct((n_slots, d), hn.dtype),
        input_output_aliases={3: 0},
        compiler_params=pltpu.CompilerParams(dimension_semantics=("arbitrary",), has_side_effects=True),
        name="moe_dispatch",
    )(dest0, dest1, hn, xs0)


def _moe_kernel(be_ref, nu_ref, xs_ref, wg_ref, wu_ref, wd_ref, ys_ref, xb_ref):
    del be_ref
    f = pl.program_id(1)

    @pl.when(pl.program_id(0) < nu_ref[0])
    def _():
        @pl.when(f == 0)
        def _():
            xb_ref[...] = xs_ref[...].astype(BF16)

        x = xb_ref[...]
        g = jnp.dot(x, wg_ref[0], preferred_element_type=F32)
        u = jnp.dot(x, wu_ref[0], preferred_element_type=F32)
        y = jnp.dot((_silu(g) * u).astype(BF16), wd_ref[0], preferred_element_type=F32)

        @pl.when(f == 0)
        def _():
            ys_ref[...] = y

        @pl.when(f > 0)
        def _():
            ys_ref[...] += y

    @pl.when(jnp.logical_and(pl.program_id(0) >= nu_ref[0], f == 0))
    def _():
        ys_ref[...] = jnp.zeros_like(ys_ref)


def _stage_moe(blk_e, n_used, xs, w_gate, w_up, w_down):
    n_slots, d = xs.shape
    d_ff = w_gate.shape[-1]
    tm = MOE_ROWS
    tf = _tile(d_ff, MOE_COLS)
    nb, nf = n_slots // tm, d_ff // tf

    def blk(b, nu):
        return jnp.minimum(b, nu[0] - 1)

    def col(b, f, nu):
        return jnp.where(b < nu[0], f, nf - 1)

    return pl.pallas_call(
        _moe_kernel,
        grid_spec=pltpu.PrefetchScalarGridSpec(
            num_scalar_prefetch=2, grid=(nb, nf),
            in_specs=[pl.BlockSpec((tm, d), lambda b, f, be, nu: (blk(b, nu), 0)),
                      pl.BlockSpec((1, d, tf), lambda b, f, be, nu: (be[blk(b, nu)], 0, col(b, f, nu))),
                      pl.BlockSpec((1, d, tf), lambda b, f, be, nu: (be[blk(b, nu)], 0, col(b, f, nu))),
                      pl.BlockSpec((1, tf, d), lambda b, f, be, nu: (be[blk(b, nu)], col(b, f, nu), 0))],
            out_specs=pl.BlockSpec((tm, d), lambda b, f, be, nu: (b, 0)),
            scratch_shapes=[pltpu.VMEM((tm, d), BF16)]),
        out_shape=jax.ShapeDtypeStruct((n_slots, d), F32),
        compiler_params=_params(("arbitrary", "arbitrary")),
        name="moe_experts",
    )(blk_e, n_used, xs, w_gate.astype(BF16), w_up.astype(BF16), w_down.astype(BF16))


def _combine_kernel(d0_ref, d1_ref, h_ref, w_ref, nw_ref, ys_ref, o_ref, y0_ref, y1_ref, sem, *, tc):
    base = pl.program_id(0) * tc

    def issue(t, carry):
        _row_copy(ys_ref, d0_ref[base + t], y0_ref, t, sem).start()
        _row_copy(ys_ref, d1_ref[base + t], y1_ref, t, sem).start()
        return carry

    def drain(t, carry):
        _row_copy(ys_ref, 0, y0_ref, 0, sem).wait()
        _row_copy(ys_ref, 0, y1_ref, 0, sem).wait()
        return carry

    lax.fori_loop(0, tc, issue, 0, unroll=8)
    lax.fori_loop(0, tc, drain, 0, unroll=8)
    h = h_ref[...] + (y0_ref[...] * w_ref[:, 0:1] + y1_ref[...] * w_ref[:, 1:2])
    o_ref[...] = _rms(h, nw_ref[...])


def _stage_combine(dest0, dest1, h, wcol, norm_w, ys):
    t, d = h.shape
    tc = _tile(t, DMA_ROWS)
    return pl.pallas_call(
        functools.partial(_combine_kernel, tc=tc),
        grid_spec=pltpu.PrefetchScalarGridSpec(
            num_scalar_prefetch=2, grid=(t // tc,),
            in_specs=[pl.BlockSpec((tc, d), lambda i, *_: (i, 0)),
                      pl.BlockSpec((tc, LANES), lambda i, *_: (i, 0)),
                      pl.BlockSpec((1, d), lambda i, *_: (0, 0)),
                      pl.BlockSpec(memory_space=pl.ANY)],
            out_specs=pl.BlockSpec((tc, d), lambda i, *_: (i, 0)),
            scratch_shapes=[pltpu.VMEM((tc, d), F32), pltpu.VMEM((tc, d), F32),
                            pltpu.SemaphoreType.DMA]),
        out_shape=jax.ShapeDtypeStruct((t, d), F32),
        compiler_params=_params(("arbitrary",)),
        name="moe_combine",
    )(dest0, dest1, h, wcol, norm_w.reshape(1, d), ys)


def kernel(x, attn_norm_w, ffn_norm_w, a_w_in, a_conv_w, a_a_log, a_dt_bias, a_onorm_w, a_w_out,
           kv_norm_w, w_kv, b_w_q, b_sinks, b_w_o, rel_bias, ffn_w_gate, ffn_w_up, ffn_w_down,
           moe_router, moe_w_gate, moe_w_up, moe_w_down, final_norm_w):
    bsz, L, d = x.shape
    t = bsz * L

    q, k, v, gate, bg, gct = _stage_in(x, attn_norm_w[0], a_w_in[0], a_conv_w[0], a_a_log[0],
                                       a_dt_bias[0])
    og = _stage_delta(q, k, v, gate, bg, gct, a_onorm_w[0])
    h, q2, kv2 = _stage_ffn(og.reshape(t, A_WIDTH), x.reshape(t, d), a_w_out[0], ffn_norm_w[0],
                            ffn_w_gate[0], ffn_w_up[0], ffn_w_down[0], attn_norm_w[1], kv_norm_w,
                            b_w_q[0], w_kv)

    bias4 = _stage_bias(rel_bias)
    h = _stage_attn(q2, kv2, bias4, b_sinks[0], b_w_o[0], h.reshape(bsz, L, d)).reshape(t, d)

    hn, ei, wcol, cnt = _stage_router(h, ffn_norm_w[1], moe_router[0])
    counts = cnt[:N_EXPERTS, 0].astype(I32)
    p_counts = (counts + MOE_ROWS - 1) // MOE_ROWS * MOE_ROWS
    p_end = jnp.cumsum(p_counts)
    p_start = p_end - p_counts
    dest0 = p_start[ei[0]] + ei[2]
    dest1 = p_start[ei[1]] + ei[3]
    n_blocks = -(-(2 * t) // MOE_ROWS) + N_EXPERTS
    blk_e = jnp.minimum(jnp.searchsorted(p_end, jnp.arange(n_blocks, dtype=I32) * MOE_ROWS, side='right'),
                        N_EXPERTS - 1).astype(I32)
    n_used = (p_end[-1:] // MOE_ROWS).astype(I32)
    xs = _stage_dispatch(dest0, dest1, hn, n_blocks * MOE_ROWS)
    ys = _stage_moe(blk_e, n_used, xs, moe_w_gate[0], moe_w_up[0], moe_w_down[0])
    out = _stage_combine(dest0, dest1, h, wcol, final_norm_w, ys)
    return out.reshape(bsz, L, d)
```

```python
import functools
import math

import numpy as np
import jax
import jax.numpy as jnp
from jax import lax
from jax.experimental import pallas as pl
from jax.experimental.pallas import tpu as pltpu

F32 = jnp.float32
BF16 = jnp.bfloat16
I32 = jnp.int32

EPS = 1e-6
CHUNK = 64
A_HEADS = 8
A_HEAD_DIM = 128
A_WIDTH = A_HEADS * A_HEAD_DIM
A_CONV = 4
B_Q_HEADS = 16
B_KV_HEADS = 4
B_HEAD_DIM = 64
B_GROUP = B_Q_HEADS // B_KV_HEADS
WINDOW_CHUNKS = 2
BAND = (WINDOW_CHUNKS + 1) * CHUNK
REL_BUCKETS = 32
REL_MAX_DIST = 128
N_EXPERTS = 8
LANES = 128

VMEM_LIMIT = 56 * 1024 * 1024

ROW_TILE = 512
CONV_COLS = 512
FFN_COLS = 256
MOE_ROWS = 1024
MOE_COLS = 1792
MOE_SUB = 256
ATT_Q = 2 * CHUNK
LOCAL_CHUNKS = 2
DMA_ROWS = 512


def _tile(n, pref):
    t = min(n, pref)
    assert n % t == 0, (n, t)
    return t


def _mm(a, b):
    return jnp.dot(a.astype(BF16), b.astype(BF16), preferred_element_type=F32)


def _mm_nt(a, b):
    return lax.dot_general(a.astype(BF16), b.astype(BF16), (((1,), (1,)), ((), ())),
                           preferred_element_type=F32)


def _mm_tn(a, b):
    return lax.dot_general(a.astype(BF16), b.astype(BF16), (((0,), (0,)), ((), ())),
                           preferred_element_type=F32)


def _silu(x):
    return x * jax.nn.sigmoid(x)


def _rms(x, w):
    return x * lax.rsqrt(jnp.mean(x * x, axis=-1, keepdims=True) + EPS) * w


def _const_spec(shape):
    nd = len(shape)
    return pl.BlockSpec(shape, lambda *_: (0,) * nd, pipeline_mode=pl.Buffered(1))


def _params(sem):
    return pltpu.CompilerParams(dimension_semantics=sem, vmem_limit_bytes=VMEM_LIMIT)


def _in_kernel(x_ref, nw_ref, w_ref, wba_ref, cw_ref, ap_ref, seg_ref,
               q_ref, k_ref, v_ref, gate_ref, bg_ref, gct_ref, ext_ref, *, tm):
    @pl.when(pl.program_id(1) == 0)
    def _():
        ext_ref[0:8, :] = jnp.zeros((8, 3 * A_WIDTH), F32)

    hn = _rms(x_ref[0], nw_ref[...]).astype(BF16)
    outs = (q_ref, k_ref, v_ref)
    for c in range(3 * A_WIDTH // CONV_COLS):
        cs = slice(c * CONV_COLS, (c + 1) * CONV_COLS)
        p = jnp.dot(hn, w_ref[:, cs], preferred_element_type=F32)
        ext_ref[8:tm + 8, cs] = p
        acc = cw_ref[A_CONV - 1:A_CONV, cs] * p
        for j in range(A_CONV - 1):
            acc = acc + cw_ref[j:j + 1, cs] * ext_ref[5 + j:5 + j + tm, cs]
        ext_ref[0:8, cs] = p[tm - 8:, :]
        a = _silu(acc)
        which, off = divmod(c * CONV_COLS, A_WIDTH)
        for hh in range(CONV_COLS // A_HEAD_DIM):
            hs = a[:, hh * A_HEAD_DIM:(hh + 1) * A_HEAD_DIM]
            if which < 2:
                hs = hs * lax.rsqrt(jnp.sum(hs * hs, axis=-1, keepdims=True) + EPS)
            if which == 0:
                hs = hs * (A_HEAD_DIM ** -0.5)
            lo = off + hh * A_HEAD_DIM
            outs[which][0, :, lo:lo + A_HEAD_DIM] = hs.astype(BF16)

    for c in range(A_WIDTH // CONV_COLS):
        cs = slice(3 * A_WIDTH + c * CONV_COLS, 3 * A_WIDTH + (c + 1) * CONV_COLS)
        gate_ref[0, :, c * CONV_COLS:(c + 1) * CONV_COLS] = jnp.dot(
            hn, w_ref[:, cs], preferred_element_type=F32)

    ba = jnp.dot(hn, wba_ref[...], preferred_element_type=F32)
    beta = jax.nn.sigmoid(ba)
    z = ba + ap_ref[1:2, :]
    softplus = jnp.maximum(z, 0.0) + jnp.log1p(jnp.exp(-jnp.abs(z)))
    g = -jnp.exp(ap_ref[0:1, :]) * softplus
    g1 = g.astype(BF16)
    r1 = g - g1.astype(F32)
    g2 = r1.astype(BF16)
    g3 = (r1 - g2.astype(F32)).astype(BF16)
    seg = seg_ref[...]
    gc = (jnp.dot(seg, g1, preferred_element_type=F32) + jnp.dot(seg, g2, preferred_element_type=F32)
          + jnp.dot(seg, g3, preferred_element_type=F32))
    lane = lax.broadcasted_iota(I32, (tm, LANES), 1)
    bg = jnp.where(lane < A_HEADS, beta, gc)
    bg_ref[0] = bg
    bgt = bg.T
    for ci in range(tm // CHUNK):
        gct_ref[0, ci] = bgt[0:2 * A_HEADS, ci * CHUNK:(ci + 1) * CHUNK]


def _stage_in(x, norm_w, w_in, conv_w, a_log, dt_bias):
    bsz, L, d = x.shape
    tm = _tile(L, ROW_TILE)
    nc = L // CHUNK
    w_main = w_in[:, :4 * A_WIDTH].astype(BF16)
    w_ba = jnp.pad(w_in[:, 4 * A_WIDTH:], ((0, 0), (0, LANES - 2 * A_HEADS))).astype(BF16)
    ap = jnp.zeros((2, LANES), F32)
    ap = ap.at[0, A_HEADS:2 * A_HEADS].set(a_log.astype(F32))
    ap = ap.at[1, A_HEADS:2 * A_HEADS].set(dt_bias.astype(F32))
    r = np.arange(tm)
    seg = jnp.asarray(((r[:, None] >= r[None, :]) &
                       (r[:, None] // CHUNK == r[None, :] // CHUNK)).astype(np.float32)).astype(BF16)
    row_spec = lambda w: pl.BlockSpec((1, tm, w), lambda b, i: (b, i, 0))
    act = jax.ShapeDtypeStruct((bsz, L, A_WIDTH), BF16)
    return pl.pallas_call(
        functools.partial(_in_kernel, tm=tm),
        grid=(bsz, L // tm),
        in_specs=[row_spec(d), _const_spec((1, d)), _const_spec((d, 4 * A_WIDTH)),
                  _const_spec((d, LANES)), _const_spec((A_CONV, 3 * A_WIDTH)),
                  _const_spec((2, LANES)), _const_spec((tm, tm))],
        out_specs=[row_spec(A_WIDTH), row_spec(A_WIDTH), row_spec(A_WIDTH), row_spec(A_WIDTH),
                   row_spec(LANES),
                   pl.BlockSpec((1, tm // CHUNK, 2 * A_HEADS, CHUNK), lambda b, i: (b, i, 0, 0))],
        out_shape=[act, act, act, jax.ShapeDtypeStruct((bsz, L, A_WIDTH), F32),
                   jax.ShapeDtypeStruct((bsz, L, LANES), F32),
                   jax.ShapeDtypeStruct((bsz, nc, 2 * A_HEADS, CHUNK), F32)],
        scratch_shapes=[pltpu.VMEM((tm + 8, 3 * A_WIDTH), F32)],
        compiler_params=_params(("arbitrary", "arbitrary")),
        name="in_proj",
    )(x, norm_w.reshape(1, d), w_main, w_ba, conv_w.astype(F32), ap, seg)


def _delta_kernel(q_ref, k_ref, v_ref, gate_ref, bg_ref, gct_ref, ow_ref, o_ref,
                  s_ref, u_ref, w_ref, qd_ref, kd_ref, at_ref, *, cb):
    @pl.when(pl.program_id(1) == 0)
    def _():
        s_ref[...] = jnp.zeros_like(s_ref)

    heads = range(A_HEADS)
    hcols = [slice(h * A_HEAD_DIM, (h + 1) * A_HEAD_DIM) for h in heads]
    row = lax.broadcasted_iota(I32, (CHUNK, CHUNK), 0)
    col = lax.broadcasted_iota(I32, (CHUNK, CHUNK), 1)
    tri = row >= col
    strict = row > col
    diag = (row >> 4) == (col >> 4)
    eye = (row == col).astype(F32)

    def local(cc, carry):
        items = [(j, h) for j in range(LOCAL_CHUNKS) for h in heads]
        its = range(len(items))
        cidx = [cc * LOCAL_CHUNKS + j for j in range(LOCAL_CHUNKS)]
        crow = [pl.ds(pl.multiple_of(c * CHUNK, CHUNK), CHUNK) for c in cidx]
        rows = [crow[j] for j, _ in items]
        cols = [hcols[h] for _, h in items]
        qb = [q_ref[0, rows[i], cols[i]] for i in its]
        kb = [k_ref[0, rows[i], cols[i]] for i in its]
        vb = [v_ref[0, rows[i], cols[i]] for i in its]
        bgs = [bg_ref[0, crow[j], :] for j in range(LOCAL_CHUNKS)]
        gts = [gct_ref[0, c] for c in cidx]
        beta = [bgs[j][:, h:h + 1] for j, h in items]
        gcol = [bgs[j][:, A_HEADS + h:A_HEADS + h + 1] for j, h in items]
        grow = [gts[j][A_HEADS + h:A_HEADS + h + 1, :] for j, h in items]

        decay = [jnp.exp(jnp.where(tri, gcol[i] - grow[i], -jnp.inf)) for i in its]
        eg = [jnp.exp(gcol[i]) for i in its]
        kf = [kb[i].astype(F32) for i in its]
        kbeta = [kf[i] * beta[i] for i in its]
        lower = [jnp.where(strict, _mm_nt(kbeta[i], kb[i]) * decay[i], 0.0) for i in its]
        ld = [jnp.where(diag, lower[i], 0.0) for i in its]
        lo = [lower[i] - ld[i] for i in its]
        p = [eye - ld[i] for i in its]
        s = [_mm(ld[i], ld[i]) for i in its]
        for _ in range(2):
            p = [p[i] + _mm(p[i], s[i]) for i in its]
            s = [_mm(s[i], s[i]) for i in its]
        td = [p[i] + _mm(p[i], s[i]) for i in its]
        rhs = [jnp.concatenate([vb[i].astype(F32) * beta[i], kbeta[i] * eg[i]], axis=1) for i in its]
        r = [_mm(td[i], rhs[i]) for i in its]
        m = [_mm(td[i], lo[i]) for i in its]
        m2 = [_mm(m[i], m[i]) for i in its]
        r = [r[i] + _mm(m2[i], r[i]) for i in its]
        uw = [r[i] - _mm(m[i], r[i]) for i in its]
        attn = [_mm_nt(qb[i], kb[i]) * decay[i] for i in its]
        qd = [qb[i].astype(F32) * eg[i] for i in its]
        kd = [kf[i] * jnp.exp(grow[i][:, CHUNK - 1:CHUNK] - gcol[i]) for i in its]

        for i, (_, h) in enumerate(items):
            u_ref[rows[i], cols[i]] = uw[i][:, :A_HEAD_DIM]
            w_ref[rows[i], cols[i]] = uw[i][:, A_HEAD_DIM:].astype(BF16)
            qd_ref[rows[i], cols[i]] = qd[i].astype(BF16)
            kd_ref[rows[i], cols[i]] = kd[i].astype(BF16)
            at_ref[h, rows[i], :] = attn[i].astype(BF16)
        return carry

    def recur(c, carry):
        rows = pl.ds(pl.multiple_of(c * CHUNK, CHUNK), CHUNK)
        wq = [jnp.concatenate([w_ref[rows, hcols[h]], qd_ref[rows, hcols[h]]], axis=0) for h in heads]
        u = [u_ref[rows, hcols[h]] for h in heads]
        kd = [kd_ref[rows, hcols[h]] for h in heads]
        at = [at_ref[h, rows, :] for h in heads]
        st = [s_ref[h] for h in heads]
        gate = [gate_ref[0, rows, hcols[h]] for h in heads]
        gts = gct_ref[0, c]
        eglast = [jnp.exp(gts[A_HEADS + h:A_HEADS + h + 1, CHUNK - 1:CHUNK]) for h in heads]
        ow = ow_ref[...]

        ws_qs = [_mm(wq[h], st[h]) for h in heads]
        vnew = [u[h] - ws_qs[h][:CHUNK] for h in heads]
        o = [ws_qs[h][CHUNK:] + _mm(at[h], vnew[h]) for h in heads]
        snew = [st[h] * eglast[h] + _mm_tn(kd[h], vnew[h]) for h in heads]
        og = [_rms(o[h], ow) * _silu(gate[h]) for h in heads]

        for h in heads:
            s_ref[h] = snew[h]
            o_ref[0, rows, hcols[h]] = og[h].astype(BF16)
        return carry

    lax.fori_loop(0, cb // LOCAL_CHUNKS, local, 0)
    lax.fori_loop(0, cb, recur, 0)


def _stage_delta(q, k, v, gate, bg, gct, onorm_w):
    bsz, L, _ = q.shape
    tm = _tile(L, ROW_TILE)
    cb = tm // CHUNK
    row_spec = lambda w: pl.BlockSpec((1, tm, w), lambda b, i: (b, i, 0))
    return pl.pallas_call(
        functools.partial(_delta_kernel, cb=cb),
        grid=(bsz, L // tm),
        in_specs=[row_spec(A_WIDTH), row_spec(A_WIDTH), row_spec(A_WIDTH), row_spec(A_WIDTH),
                  row_spec(LANES),
                  pl.BlockSpec((1, cb, 2 * A_HEADS, CHUNK), lambda b, i: (b, i, 0, 0)),
                  _const_spec((1, A_HEAD_DIM))],
        out_specs=row_spec(A_WIDTH),
        out_shape=jax.ShapeDtypeStruct((bsz, L, A_WIDTH), BF16),
        scratch_shapes=[pltpu.VMEM((A_HEADS, A_HEAD_DIM, A_HEAD_DIM), F32),
                        pltpu.VMEM((tm, A_WIDTH), F32), pltpu.VMEM((tm, A_WIDTH), BF16),
                        pltpu.VMEM((tm, A_WIDTH), BF16), pltpu.VMEM((tm, A_WIDTH), BF16),
                        pltpu.VMEM((A_HEADS, tm, CHUNK), BF16)],
        compiler_params=_params(("arbitrary", "arbitrary")),
        name="delta_rule",
    )(q, k, v, gate, bg, gct, onorm_w.reshape(1, A_HEAD_DIM).astype(F32))


def _ffn_kernel(og_ref, x_ref, wo_ref, fw_ref, wg_ref, wu_ref, wd_ref, aw_ref, kw_ref, wq_ref,
                wkv_ref, h_ref, q_ref, kv_ref, *, d_ff):
    h1 = x_ref[...] + jnp.dot(og_ref[...], wo_ref[...], preferred_element_type=F32)
    hn = _rms(h1, fw_ref[...]).astype(BF16)
    acc = jnp.zeros_like(h1)
    for c in range(d_ff // FFN_COLS):
        cs = slice(c * FFN_COLS, (c + 1) * FFN_COLS)
        g = jnp.dot(hn, wg_ref[:, cs], preferred_element_type=F32)
        u = jnp.dot(hn, wu_ref[:, cs], preferred_element_type=F32)
        acc = acc + jnp.dot((_silu(g) * u).astype(BF16), wd_ref[cs, :],
                            preferred_element_type=F32)
    h2 = h1 + acc
    h_ref[...] = h2
    xh = h2 * lax.rsqrt(jnp.mean(h2 * h2, axis=-1, keepdims=True) + EPS)
    q = jnp.dot((xh * aw_ref[...]).astype(BF16), wq_ref[...], preferred_element_type=F32)
    q_ref[...] = (q * (B_HEAD_DIM ** -0.5)).astype(BF16)
    kv_ref[...] = jnp.dot((xh * kw_ref[...]).astype(BF16), wkv_ref[...],
                          preferred_element_type=F32).astype(BF16)


def _stage_ffn(og, x, w_out, ffn_norm_w, w_gate, w_up, w_down, attn_norm_w, kv_norm_w, w_q, w_kv):
    t, d = x.shape
    d_ff = w_gate.shape[1]
    tm = _tile(t, ROW_TILE)
    hq = B_Q_HEADS * B_HEAD_DIM
    half = B_KV_HEADS * B_HEAD_DIM
    dup = lambda w: jnp.concatenate([w.reshape(d, B_KV_HEADS, 1, B_HEAD_DIM)] * 2, axis=2).reshape(d, 2 * half)
    w_kvd = jnp.concatenate([dup(w_kv[:, :half]), dup(w_kv[:, half:])], axis=1).astype(BF16)
    row_spec = lambda w: pl.BlockSpec((tm, w), lambda i: (i, 0))
    return pl.pallas_call(
        functools.partial(_ffn_kernel, d_ff=d_ff),
        grid=(t // tm,),
        in_specs=[row_spec(A_WIDTH), row_spec(d), _const_spec((A_WIDTH, d)), _const_spec((1, d)),
                  _const_spec((d, d_ff)), _const_spec((d, d_ff)), _const_spec((d_ff, d)),
                  _const_spec((1, d)), _const_spec((1, d)), _const_spec((d, hq)),
                  _const_spec((d, 4 * half))],
        out_specs=[row_spec(d), row_spec(hq), row_spec(4 * half)],
        out_shape=[jax.ShapeDtypeStruct((t, d), F32), jax.ShapeDtypeStruct((t, hq), BF16),
                   jax.ShapeDtypeStruct((t, 4 * half), BF16)],
        compiler_params=_params(("arbitrary",)),
        name="ffn_dense",
    )(og, x, w_out.astype(BF16), ffn_norm_w.reshape(1, d), w_gate.astype(BF16), w_up.astype(BF16),
      w_down.astype(BF16), attn_norm_w.reshape(1, d), kv_norm_w.reshape(1, d), w_q.astype(BF16), w_kvd)


def _rel_buckets(rel):
    nb = REL_BUCKETS // 2
    max_exact = nb // 2
    ret = jnp.where(rel > 0, nb, 0)
    dist = jnp.abs(rel)
    dist_f = jnp.maximum(dist, 1).astype(F32)
    large = max_exact + (jnp.log(dist_f / max_exact) / math.log(REL_MAX_DIST / max_exact)
                         * (nb - max_exact)).astype(I32)
    large = jnp.minimum(large, nb - 1)
    return ret + jnp.where(dist < max_exact, dist, large)


def _bias_kernel(rb_ref, bk_ref, o_ref):
    bk = bk_ref[...]
    for h in range(B_Q_HEADS):
        acc = jnp.zeros(bk.shape, F32)
        for b in range(REL_BUCKETS):
            acc = jnp.where(bk == b, rb_ref[b * B_Q_HEADS + h], acc)
        o_ref[h] = acc


def _stage_bias(rel_bias):
    rel = (jnp.arange(BAND)[None, :] - WINDOW_CHUNKS * CHUNK) - jnp.arange(CHUNK)[:, None]
    buckets = _rel_buckets(rel).astype(I32)
    bias = pl.pallas_call(
        _bias_kernel,
        in_specs=[pl.BlockSpec(memory_space=pltpu.SMEM), pl.BlockSpec(memory_space=pltpu.VMEM)],
        out_specs=pl.BlockSpec(memory_space=pltpu.VMEM),
        out_shape=jax.ShapeDtypeStruct((B_Q_HEADS, CHUNK, BAND), F32),
        name="rel_bias",
    )(rel_bias.astype(F32).reshape(-1), buckets)
    pad = lambda lo, hi: jnp.pad(bias, ((0, 0), (0, 0), (lo, hi)), constant_values=-jnp.inf)
    pair = jnp.concatenate([pad(0, CHUNK), pad(CHUNK, 0)], axis=1)
    return pair.reshape(B_KV_HEADS, B_GROUP * ATT_Q, 2 * ATT_Q)


def _attn_kernel(q_ref, kv_ref, kvp_ref, bias_ref, sink_ref, wo_ref, h_ref, o_ref, att_ref, *, nblk):
    first = pl.program_id(1) == 0
    lane = lax.broadcasted_iota(I32, (ATT_Q, LANES), 1)
    lo_half = lane < B_HEAD_DIM
    kidx = lax.broadcasted_iota(I32, (B_GROUP * ATT_Q, 2 * ATT_Q), 1)
    koff = B_KV_HEADS * LANES
    for jb in range(nblk):
        rows = slice(jb * ATT_Q, (jb + 1) * ATT_Q)
        if jb == 0:
            keys = jnp.concatenate([kvp_ref[0], kv_ref[0, :ATT_Q, :]], axis=0)
        else:
            keys = kv_ref[0, (jb - 1) * ATT_Q:(jb + 1) * ATT_Q, :]
        for hk in range(B_KV_HEADS):
            kd = keys[:, hk * LANES:(hk + 1) * LANES]
            vd = keys[:, koff + hk * LANES:koff + (hk + 1) * LANES]
            qa = q_ref[0, rows, (2 * hk) * LANES:(2 * hk + 1) * LANES]
            qb = q_ref[0, rows, (2 * hk + 1) * LANES:(2 * hk + 2) * LANES]
            zero = jnp.zeros_like(qa)
            q4 = jnp.concatenate([jnp.where(lo_half, qa, zero), jnp.where(lo_half, zero, qa),
                                  jnp.where(lo_half, qb, zero), jnp.where(lo_half, zero, qb)], axis=0)
            s = lax.dot_general(q4, kd, (((1,), (1,)), ((), ())), preferred_element_type=F32)
            s = s + bias_ref[hk]
            if jb == 0:
                s = jnp.where(jnp.logical_and(first, kidx < ATT_Q), -jnp.inf, s)
            sink = sink_ref[hk]
            m = jnp.maximum(jnp.max(s, axis=-1, keepdims=True), sink)
            p = jnp.exp(s - m)
            denom = jnp.sum(p, axis=-1, keepdims=True) + jnp.exp(sink - m)
            o = jnp.dot(p.astype(BF16), vd, preferred_element_type=F32) / denom
            ta = jnp.where(lo_half, o[0:ATT_Q], o[ATT_Q:2 * ATT_Q])
            tb = jnp.where(lo_half, o[2 * ATT_Q:3 * ATT_Q], o[3 * ATT_Q:4 * ATT_Q])
            att_ref[rows, (2 * hk) * LANES:(2 * hk + 1) * LANES] = ta.astype(BF16)
            att_ref[rows, (2 * hk + 1) * LANES:(2 * hk + 2) * LANES] = tb.astype(BF16)
    o_ref[0] = h_ref[0] + jnp.dot(att_ref[...], wo_ref[...], preferred_element_type=F32)


def _stage_attn(q, kv, bias4, sinks, w_o, h):
    bsz, L, d = h.shape
    tq = _tile(L, ROW_TILE)
    halo = WINDOW_CHUNKS * CHUNK
    per = tq // halo
    hq = B_Q_HEADS * B_HEAD_DIM
    sink4 = jnp.repeat(sinks.astype(F32).reshape(B_KV_HEADS, B_GROUP), ATT_Q, axis=1)[..., None]
    row_spec = lambda w: pl.BlockSpec((1, tq, w), lambda b, i: (b, i, 0))
    return pl.pallas_call(
        functools.partial(_attn_kernel, nblk=tq // ATT_Q),
        grid=(bsz, L // tq),
        in_specs=[row_spec(hq), row_spec(kv.shape[-1]),
                  pl.BlockSpec((1, halo, kv.shape[-1]), lambda b, i: (b, jnp.maximum(i * per - 1, 0), 0)),
                  _const_spec(bias4.shape), _const_spec(sink4.shape), _const_spec((hq, d)),
                  row_spec(d)],
        out_specs=row_spec(d),
        out_shape=jax.ShapeDtypeStruct((bsz, L, d), F32),
        scratch_shapes=[pltpu.VMEM((tq, hq), BF16)],
        compiler_params=_params(("arbitrary", "arbitrary")),
        name="swa_attn",
    )(q.reshape(bsz, L, hq), kv.reshape(bsz, L, -1), kv.reshape(bsz, L, -1), bias4, sink4,
      w_o.astype(BF16), h)


def _router_kernel(h_ref, nw_ref, rw_ref, tri_ref, ei_ref, wcol_ref, cnt_ref, carry_ref, *, tr):
    @pl.when(pl.program_id(0) == 0)
    def _():
        carry_ref[...] = jnp.zeros_like(carry_ref)

    hn = _rms(h_ref[...], nw_ref[...])
    rows = 2 * N_EXPERTS
    lt = lax.dot_general(rw_ref[...], hn.astype(BF16), (((1,), (1,)), ((), ())),
                         preferred_element_type=F32)
    sub = lax.broadcasted_iota(I32, (rows, tr), 0)
    lt = jnp.where(sub < N_EXPERTS, lt, -jnp.inf)
    m1 = jnp.max(lt, axis=0, keepdims=True)
    i1 = jnp.min(jnp.where(lt == m1, sub, rows), axis=0, keepdims=True)
    lt2 = jnp.where(sub == i1, -jnp.inf, lt)
    m2 = jnp.max(lt2, axis=0, keepdims=True)
    i2 = jnp.min(jnp.where(lt2 == m2, sub, rows), axis=0, keepdims=True)
    e2 = jnp.exp(m2 - m1)
    w1 = 1.0 / (1.0 + e2)
    w2 = e2 / (1.0 + e2)
    hit = jnp.logical_or(sub == i1, sub == i2).astype(F32)
    pref = jnp.dot(hit.astype(BF16), tri_ref[...], preferred_element_type=F32) + carry_ref[:, 0:1]
    r1 = jnp.sum(jnp.where(sub == i1, pref, 0.0), axis=0, keepdims=True)
    r2 = jnp.sum(jnp.where(sub == i2, pref, 0.0), axis=0, keepdims=True)
    total = carry_ref[...] + jnp.sum(hit, axis=1, keepdims=True)
    carry_ref[...] = total
    cnt_ref[...] = total
    ei = jnp.where(sub == 0, i1, jnp.where(sub == 1, i2, jnp.where(
        sub == 2, r1.astype(I32), jnp.where(sub == 3, r2.astype(I32), 0))))
    ei_ref[...] = ei[0:8]
    wrow = jnp.where(sub == 0, w1, jnp.where(sub == 1, w2, 0.0))
    wpad = jnp.concatenate([wrow, jnp.zeros((LANES - rows, tr), F32)], axis=0)
    wcol_ref[...] = wpad.T


def _stage_router(h, norm_w, router_w):
    t, d = h.shape
    tr = _tile(t, ROW_TILE)
    rows = 2 * N_EXPERTS
    rwt = jnp.pad(router_w.T, ((0, rows - N_EXPERTS), (0, 0))).astype(BF16)
    r = np.arange(tr)
    tri = jnp.asarray((r[:, None] < r[None, :]).astype(np.float32)).astype(BF16)
    return pl.pallas_call(
        functools.partial(_router_kernel, tr=tr),
        grid=(t // tr,),
        in_specs=[pl.BlockSpec((tr, d), lambda i: (i, 0)), _const_spec((1, d)),
                  _const_spec((rows, d)), _const_spec((tr, tr))],
        out_specs=[pl.BlockSpec((8, tr), lambda i: (0, i)),
                   pl.BlockSpec((tr, LANES), lambda i: (i, 0)),
                   pl.BlockSpec((rows, LANES), lambda i: (0, 0))],
        out_shape=[jax.ShapeDtypeStruct((8, t), I32),
                   jax.ShapeDtypeStruct((t, LANES), F32), jax.ShapeDtypeStruct((rows, LANES), F32)],
        scratch_shapes=[pltpu.VMEM((rows, LANES), F32)],
        compiler_params=_params(("arbitrary",)),
        name="router",
    )(h, norm_w.reshape(1, d), rwt, tri)


def _row_copy(src_ref, src_row, dst_ref, dst_row, sem):
    return pltpu.make_async_copy(src_ref.at[pl.ds(src_row, 1)], dst_ref.at[pl.ds(dst_row, 1)], sem)


def _dispatch_kernel(d0_ref, d1_ref, h_ref, nw_ref, xs_in_ref, xs_ref, hn_ref, sem, *, td):
    del xs_in_ref
    base = pl.program_id(0) * td
    hn_ref[...] = _rms(h_ref[...], nw_ref[...])

    def issue(t, carry):
        _row_copy(hn_ref, t, xs_ref, d0_ref[base + t], sem).start(priority=0)
        _row_copy(hn_ref, t, xs_ref, d1_ref[base + t], sem).start(priority=1)
        return carry

    def drain(t, carry):
        _row_copy(hn_ref, 0, xs_ref, 0, sem).wait()
        _row_copy(hn_ref, 0, xs_ref, 0, sem).wait()
        return carry

    lax.fori_loop(0, td, issue, 0, unroll=8)
    lax.fori_loop(0, td, drain, 0, unroll=8)


def _stage_dispatch(dest0, dest1, h, norm_w, n_slots):
    t, d = h.shape
    td = _tile(t, DMA_ROWS)
    xs0 = jnp.zeros((n_slots, d), F32)
    return pl.pallas_call(
        functools.partial(_dispatch_kernel, td=td),
        grid_spec=pltpu.PrefetchScalarGridSpec(
            num_scalar_prefetch=2, grid=(t // td,),
            in_specs=[pl.BlockSpec((td, d), lambda i, *_: (i, 0)),
                      pl.BlockSpec((1, d), lambda i, *_: (0, 0)),
                      pl.BlockSpec(memory_space=pl.ANY)],
            out_specs=pl.BlockSpec(memory_space=pl.ANY),
            scratch_shapes=[pltpu.VMEM((td, d), F32), pltpu.SemaphoreType.DMA]),
        out_shape=jax.ShapeDtypeStruct((n_slots, d), F32),
        input_output_aliases={4: 0},
        compiler_params=pltpu.CompilerParams(dimension_semantics=("arbitrary",), has_side_effects=True,
                                             vmem_limit_bytes=VMEM_LIMIT),
        name="moe_dispatch",
    )(dest0, dest1, h, norm_w.reshape(1, d), xs0)


def _moe_kernel(be_ref, nu_ref, xs_ref, wg_ref, wu_ref, wd_ref, ys_ref, xb_ref, hd_ref, *, tf):
    del be_ref
    f = pl.program_id(1)

    @pl.when(pl.program_id(0) < nu_ref[0])
    def _():
        @pl.when(f == 0)
        def _():
            xb_ref[...] = xs_ref[...].astype(BF16)

        x = xb_ref[...]
        for c in range(tf // MOE_SUB):
            cs = slice(c * MOE_SUB, (c + 1) * MOE_SUB)
            g = jnp.dot(x, wg_ref[0, :, cs], preferred_element_type=F32)
            u = jnp.dot(x, wu_ref[0, :, cs], preferred_element_type=F32)
            hd_ref[:, cs] = (_silu(g) * u).astype(BF16)
        y = jnp.dot(hd_ref[...], wd_ref[0], preferred_element_type=F32)

        @pl.when(f == 0)
        def _():
            ys_ref[...] = y

        @pl.when(f > 0)
        def _():
            ys_ref[...] += y

    @pl.when(jnp.logical_and(pl.program_id(0) >= nu_ref[0], f == 0))
    def _():
        ys_ref[...] = jnp.zeros_like(ys_ref)


def _stage_moe(blk_e, n_used, xs, w_gate, w_up, w_down):
    n_slots, d = xs.shape
    d_ff = w_gate.shape[-1]
    tm = MOE_ROWS
    tf = _tile(d_ff, MOE_COLS)
    nb, nf = n_slots // tm, d_ff // tf

    def blk(b, nu):
        return jnp.minimum(b, nu[0] - 1)

    def col(b, f, nu):
        return jnp.where(b < nu[0], f, nf - 1)

    return pl.pallas_call(
        functools.partial(_moe_kernel, tf=tf),
        grid_spec=pltpu.PrefetchScalarGridSpec(
            num_scalar_prefetch=2, grid=(nb, nf),
            in_specs=[pl.BlockSpec((tm, d), lambda b, f, be, nu: (blk(b, nu), 0)),
                      pl.BlockSpec((1, d, tf), lambda b, f, be, nu: (be[blk(b, nu)], 0, col(b, f, nu))),
                      pl.BlockSpec((1, d, tf), lambda b, f, be, nu: (be[blk(b, nu)], 0, col(b, f, nu))),
                      pl.BlockSpec((1, tf, d), lambda b, f, be, nu: (be[blk(b, nu)], col(b, f, nu), 0))],
            out_specs=pl.BlockSpec((tm, d), lambda b, f, be, nu: (b, 0)),
            scratch_shapes=[pltpu.VMEM((tm, d), BF16), pltpu.VMEM((tm, tf), BF16)]),
        out_shape=jax.ShapeDtypeStruct((n_slots, d), F32),
        compiler_params=_params(("arbitrary", "arbitrary")),
        name="moe_experts",
    )(blk_e, n_used, xs, w_gate.astype(BF16), w_up.astype(BF16), w_down.astype(BF16))


def _combine_kernel(d0_ref, d1_ref, h_ref, w_ref, nw_ref, ys_ref, o_ref, y0_ref, y1_ref, sem, *, tc):
    i = pl.program_id(0)
    n = pl.num_programs(0)

    def gather(step, slot):
        base = step * tc

        def issue(t, carry):
            _row_copy(ys_ref, d0_ref[base + t], y0_ref.at[slot], t, sem.at[slot]).start(priority=0)
            _row_copy(ys_ref, d1_ref[base + t], y1_ref.at[slot], t, sem.at[slot]).start(priority=1)
            return carry

        lax.fori_loop(0, tc, issue, 0, unroll=8)

    @pl.when(i == 0)
    def _():
        gather(0, 0)

    slot = lax.rem(i, 2)

    @pl.when(i + 1 < n)
    def _():
        gather(i + 1, 1 - slot)

    def drain(t, carry):
        _row_copy(ys_ref, 0, y0_ref.at[slot], 0, sem.at[slot]).wait()
        _row_copy(ys_ref, 0, y1_ref.at[slot], 0, sem.at[slot]).wait()
        return carry

    lax.fori_loop(0, tc, drain, 0, unroll=8)
    h = h_ref[...] + (y0_ref[slot] * w_ref[:, 0:1] + y1_ref[slot] * w_ref[:, 1:2])
    o_ref[...] = _rms(h, nw_ref[...])


def _stage_combine(dest0, dest1, h, wcol, norm_w, ys):
    t, d = h.shape
    tc = _tile(t, DMA_ROWS)
    return pl.pallas_call(
        functools.partial(_combine_kernel, tc=tc),
        grid_spec=pltpu.PrefetchScalarGridSpec(
            num_scalar_prefetch=2, grid=(t // tc,),
            in_specs=[pl.BlockSpec((tc, d), lambda i, *_: (i, 0)),
                      pl.BlockSpec((tc, LANES), lambda i, *_: (i, 0)),
                      pl.BlockSpec((1, d), lambda i, *_: (0, 0)),
                      pl.BlockSpec(memory_space=pl.ANY)],
            out_specs=pl.BlockSpec((tc, d), lambda i, *_: (i, 0)),
            scratch_shapes=[pltpu.VMEM((2, tc, d), F32), pltpu.VMEM((2, tc, d), F32),
                            pltpu.SemaphoreType.DMA((2,))]),
        out_shape=jax.ShapeDtypeStruct((t, d), F32),
        compiler_params=_params(("arbitrary",)),
        name="moe_combine",
    )(dest0, dest1, h, wcol, norm_w.reshape(1, d), ys)


def kernel(x, attn_norm_w, ffn_norm_w, a_w_in, a_conv_w, a_a_log, a_dt_bias, a_onorm_w, a_w_out,
           kv_norm_w, w_kv, b_w_q, b_sinks, b_w_o, rel_bias, ffn_w_gate, ffn_w_up, ffn_w_down,
           moe_router, moe_w_gate, moe_w_up, moe_w_down, final_norm_w):
    bsz, L, d = x.shape
    t = bsz * L

    q, k, v, gate, bg, gct = _stage_in(x, attn_norm_w[0], a_w_in[0], a_conv_w[0], a_a_log[0],
                                       a_dt_bias[0])
    og = _stage_delta(q, k, v, gate, bg, gct, a_onorm_w[0])
    h, q2, kv2 = _stage_ffn(og.reshape(t, A_WIDTH), x.reshape(t, d), a_w_out[0], ffn_norm_w[0],
                            ffn_w_gate[0], ffn_w_up[0], ffn_w_down[0], attn_norm_w[1], kv_norm_w,
                            b_w_q[0], w_kv)

    bias4 = _stage_bias(rel_bias)
    h = _stage_attn(q2, kv2, bias4, b_sinks[0], b_w_o[0], h.reshape(bsz, L, d)).reshape(t, d)

    ei, wcol, cnt = _stage_router(h, ffn_norm_w[1], moe_router[0])
    counts = cnt[:N_EXPERTS, 0].astype(I32)
    p_counts = (counts + MOE_ROWS - 1) // MOE_ROWS * MOE_ROWS
    p_end = jnp.cumsum(p_counts)
    p_start = p_end - p_counts
    dest0 = p_start[ei[0]] + ei[2]
    dest1 = p_start[ei[1]] + ei[3]
    n_blocks = -(-(2 * t) // MOE_ROWS) + N_EXPERTS
    blk_e = jnp.minimum(jnp.searchsorted(p_end, jnp.arange(n_blocks, dtype=I32) * MOE_ROWS, side='right'),
                        N_EXPERTS - 1).astype(I32)
    n_used = (p_end[-1:] // MOE_ROWS).astype(I32)
    xs = _stage_dispatch(dest0, dest1, h, ffn_norm_w[1], n_blocks * MOE_ROWS)
    ys = _stage_moe(blk_e, n_used, xs, moe_w_gate[0], moe_w_up[0], moe_w_down[0])
    out = _stage_combine(dest0, dest1, h, wcol, final_norm_w, ys)
    return out.reshape(bsz, L, d)
```

```python
import functools
import math

import numpy as np
import jax
import jax.numpy as jnp
from jax import lax
from jax.experimental import pallas as pl
from jax.experimental.pallas import tpu as pltpu

F32 = jnp.float32
BF16 = jnp.bfloat16
I32 = jnp.int32

EPS = 1e-6
CHUNK = 64
A_HEADS = 8
A_HEAD_DIM = 128
A_WIDTH = A_HEADS * A_HEAD_DIM
A_CONV = 4
B_Q_HEADS = 16
B_KV_HEADS = 4
B_HEAD_DIM = 64
B_GROUP = B_Q_HEADS // B_KV_HEADS
WINDOW_CHUNKS = 2
BAND = (WINDOW_CHUNKS + 1) * CHUNK
REL_BUCKETS = 32
REL_MAX_DIST = 128
N_EXPERTS = 8
LANES = 128

VMEM_LIMIT = 56 * 1024 * 1024

DELTA_TILE = 1024
ROW_TILE = 512
CONV_COLS = 512
FFN_COLS = 256
MOE_ROWS = 1024
MOE_COLS = 1792
SEG_ALIGN = 8
MOE_SUB = 256
ATT_Q = 2 * CHUNK
LOCAL_CHUNKS = 2


def _tile(n, pref):
    t = min(n, pref)
    assert n % t == 0, (n, t)
    return t


def _mm(a, b):
    return jnp.dot(a.astype(BF16), b.astype(BF16), preferred_element_type=F32)


def _mm_nt(a, b):
    return lax.dot_general(a.astype(BF16), b.astype(BF16), (((1,), (1,)), ((), ())),
                           preferred_element_type=F32)


def _mm_tn(a, b):
    return lax.dot_general(a.astype(BF16), b.astype(BF16), (((0,), (0,)), ((), ())),
                           preferred_element_type=F32)


def _silu(x):
    return x * jax.nn.sigmoid(x)


def _rms(x, w):
    return x * lax.rsqrt(jnp.mean(x * x, axis=-1, keepdims=True) + EPS) * w


def _const_spec(shape):
    nd = len(shape)
    return pl.BlockSpec(shape, lambda *_: (0,) * nd, pipeline_mode=pl.Buffered(1))


def _params(sem):
    return pltpu.CompilerParams(dimension_semantics=sem, vmem_limit_bytes=VMEM_LIMIT)


def _in_kernel(x_ref, nw_ref, w_ref, wba_ref, cw_ref, ap_ref, seg_ref,
               q_ref, k_ref, v_ref, gate_ref, bg_ref, gct_ref, ext_ref, *, tm):
    @pl.when(pl.program_id(1) == 0)
    def _():
        ext_ref[0:8, :] = jnp.zeros((8, 3 * A_WIDTH), F32)

    hn = _rms(x_ref[0], nw_ref[...]).astype(BF16)
    outs = (q_ref, k_ref, v_ref)
    for c in range(3 * A_WIDTH // CONV_COLS):
        cs = slice(c * CONV_COLS, (c + 1) * CONV_COLS)
        p = jnp.dot(hn, w_ref[:, cs], preferred_element_type=F32)
        ext_ref[8:tm + 8, cs] = p
        acc = cw_ref[A_CONV - 1:A_CONV, cs] * p
        for j in range(A_CONV - 1):
            acc = acc + cw_ref[j:j + 1, cs] * ext_ref[5 + j:5 + j + tm, cs]
        ext_ref[0:8, cs] = p[tm - 8:, :]
        a = _silu(acc)
        which, off = divmod(c * CONV_COLS, A_WIDTH)
        for hh in range(CONV_COLS // A_HEAD_DIM):
            hs = a[:, hh * A_HEAD_DIM:(hh + 1) * A_HEAD_DIM]
            if which < 2:
                hs = hs * lax.rsqrt(jnp.sum(hs * hs, axis=-1, keepdims=True) + EPS)
            if which == 0:
                hs = hs * (A_HEAD_DIM ** -0.5)
            lo = off + hh * A_HEAD_DIM
            outs[which][0, :, lo:lo + A_HEAD_DIM] = hs.astype(BF16)

    for c in range(A_WIDTH // CONV_COLS):
        cs = slice(3 * A_WIDTH + c * CONV_COLS, 3 * A_WIDTH + (c + 1) * CONV_COLS)
        gate_ref[0, :, c * CONV_COLS:(c + 1) * CONV_COLS] = jnp.dot(
            hn, w_ref[:, cs], preferred_element_type=F32)

    ba = jnp.dot(hn, wba_ref[...], preferred_element_type=F32)
    beta = jax.nn.sigmoid(ba)
    z = ba + ap_ref[1:2, :]
    softplus = jnp.maximum(z, 0.0) + jnp.log1p(jnp.exp(-jnp.abs(z)))
    g = -jnp.exp(ap_ref[0:1, :]) * softplus
    g1 = g.astype(BF16)
    r1 = g - g1.astype(F32)
    g2 = r1.astype(BF16)
    g3 = (r1 - g2.astype(F32)).astype(BF16)
    seg = seg_ref[...]
    gc = (jnp.dot(seg, g1, preferred_element_type=F32) + jnp.dot(seg, g2, preferred_element_type=F32)
          + jnp.dot(seg, g3, preferred_element_type=F32))
    lane = lax.broadcasted_iota(I32, (tm, LANES), 1)
    bg = jnp.where(lane < A_HEADS, beta, gc)
    bg_ref[0] = bg
    bgt = bg.T
    for ci in range(tm // CHUNK):
        gct_ref[0, ci] = bgt[0:2 * A_HEADS, ci * CHUNK:(ci + 1) * CHUNK]


def _stage_in(x, norm_w, w_in, conv_w, a_log, dt_bias):
    bsz, L, d = x.shape
    tm = _tile(L, ROW_TILE)
    nc = L // CHUNK
    w_main = w_in[:, :4 * A_WIDTH].astype(BF16)
    w_ba = jnp.pad(w_in[:, 4 * A_WIDTH:], ((0, 0), (0, LANES - 2 * A_HEADS))).astype(BF16)
    ap = jnp.zeros((2, LANES), F32)
    ap = ap.at[0, A_HEADS:2 * A_HEADS].set(a_log.astype(F32))
    ap = ap.at[1, A_HEADS:2 * A_HEADS].set(dt_bias.astype(F32))
    r = np.arange(tm)
    seg = jnp.asarray(((r[:, None] >= r[None, :]) &
                       (r[:, None] // CHUNK == r[None, :] // CHUNK)).astype(np.float32)).astype(BF16)
    row_spec = lambda w: pl.BlockSpec((1, tm, w), lambda b, i: (b, i, 0))
    act = jax.ShapeDtypeStruct((bsz, L, A_WIDTH), BF16)
    return pl.pallas_call(
        functools.partial(_in_kernel, tm=tm),
        grid=(bsz, L // tm),
        in_specs=[row_spec(d), _const_spec((1, d)), _const_spec((d, 4 * A_WIDTH)),
                  _const_spec((d, LANES)), _const_spec((A_CONV, 3 * A_WIDTH)),
                  _const_spec((2, LANES)), _const_spec((tm, tm))],
        out_specs=[row_spec(A_WIDTH), row_spec(A_WIDTH), row_spec(A_WIDTH), row_spec(A_WIDTH),
                   row_spec(LANES),
                   pl.BlockSpec((1, tm // CHUNK, 2 * A_HEADS, CHUNK), lambda b, i: (b, i, 0, 0))],
        out_shape=[act, act, act, jax.ShapeDtypeStruct((bsz, L, A_WIDTH), F32),
                   jax.ShapeDtypeStruct((bsz, L, LANES), F32),
                   jax.ShapeDtypeStruct((bsz, nc, 2 * A_HEADS, CHUNK), F32)],
        scratch_shapes=[pltpu.VMEM((tm + 8, 3 * A_WIDTH), F32)],
        compiler_params=_params(("arbitrary", "arbitrary")),
        name="in_proj",
    )(x, norm_w.reshape(1, d), w_main, w_ba, conv_w.astype(F32), ap, seg)


def _chunk_rows(c):
    if isinstance(c, int):
        return slice(c * CHUNK, (c + 1) * CHUNK)
    return pl.ds(pl.multiple_of(c * CHUNK, CHUNK), CHUNK)


def _delta_kernel(q_ref, k_ref, v_ref, gate_ref, bg_ref, gct_ref, ow_ref, o_ref,
                  s_ref, u_ref, w_ref, qd_ref, kd_ref, at_ref, *, cb):
    @pl.when(pl.program_id(1) == 0)
    def _():
        s_ref[...] = jnp.zeros_like(s_ref)

    heads = range(A_HEADS)
    hcols = [slice(h * A_HEAD_DIM, (h + 1) * A_HEAD_DIM) for h in heads]
    row = lax.broadcasted_iota(I32, (CHUNK, CHUNK), 0)
    col = lax.broadcasted_iota(I32, (CHUNK, CHUNK), 1)
    tri = row >= col
    strict = row > col
    diag = (row >> 4) == (col >> 4)
    eye = (row == col).astype(F32)

    def local(cc):
        items = [(j, h) for j in range(LOCAL_CHUNKS) for h in heads]
        its = range(len(items))
        cidx = [cc * LOCAL_CHUNKS + j for j in range(LOCAL_CHUNKS)]
        crow = [_chunk_rows(c) for c in cidx]
        rows = [crow[j] for j, _ in items]
        cols = [hcols[h] for _, h in items]
        qb = [q_ref[0, rows[i], cols[i]] for i in its]
        kb = [k_ref[0, rows[i], cols[i]] for i in its]
        vb = [v_ref[0, rows[i], cols[i]] for i in its]
        bgs = [bg_ref[0, crow[j], :] for j in range(LOCAL_CHUNKS)]
        gts = [gct_ref[0, c] for c in cidx]
        beta = [bgs[j][:, h:h + 1] for j, h in items]
        gcol = [bgs[j][:, A_HEADS + h:A_HEADS + h + 1] for j, h in items]
        grow = [gts[j][A_HEADS + h:A_HEADS + h + 1, :] for j, h in items]

        decay = [jnp.exp(jnp.where(tri, gcol[i] - grow[i], -jnp.inf)) for i in its]
        eg = [jnp.exp(gcol[i]) for i in its]
        kf = [kb[i].astype(F32) for i in its]
        kbeta = [kf[i] * beta[i] for i in its]
        lower = [jnp.where(strict, _mm_nt(kbeta[i], kb[i]) * decay[i], 0.0) for i in its]
        ld = [jnp.where(diag, lower[i], 0.0) for i in its]
        lo = [lower[i] - ld[i] for i in its]
        p = [eye - ld[i] for i in its]
        s = [_mm(ld[i], ld[i]) for i in its]
        for _ in range(2):
            p = [p[i] + _mm(p[i], s[i]) for i in its]
            s = [_mm(s[i], s[i]) for i in its]
        td = [p[i] + _mm(p[i], s[i]) for i in its]
        rhs = [jnp.concatenate([vb[i].astype(F32) * beta[i], kbeta[i] * eg[i]], axis=1) for i in its]
        r = [_mm(td[i], rhs[i]) for i in its]
        m = [_mm(td[i], lo[i]) for i in its]
        m2 = [_mm(m[i], m[i]) for i in its]
        r = [r[i] + _mm(m2[i], r[i]) for i in its]
        uw = [r[i] - _mm(m[i], r[i]) for i in its]
        attn = [_mm_nt(qb[i], kb[i]) * decay[i] for i in its]
        qd = [qb[i].astype(F32) * eg[i] for i in its]
        kd = [kf[i] * jnp.exp(grow[i][:, CHUNK - 1:CHUNK] - gcol[i]) for i in its]

        def store():
            for i, (_, h) in enumerate(items):
                u_ref[rows[i], cols[i]] = uw[i][:, :A_HEAD_DIM]
                w_ref[rows[i], cols[i]] = uw[i][:, A_HEAD_DIM:].astype(BF16)
                qd_ref[rows[i], cols[i]] = qd[i].astype(BF16)
                kd_ref[rows[i], cols[i]] = kd[i].astype(BF16)
                at_ref[h, rows[i], :] = attn[i].astype(BF16)

        return store

    def recur(cc):
        cidx = [cc * LOCAL_CHUNKS + j for j in range(LOCAL_CHUNKS)]
        crow = [_chunk_rows(c) for c in cidx]
        wq = [[jnp.concatenate([w_ref[r, hcols[h]], qd_ref[r, hcols[h]]], axis=0) for h in heads]
              for r in crow]
        u = [[u_ref[r, hcols[h]] for h in heads] for r in crow]
        kd = [[kd_ref[r, hcols[h]] for h in heads] for r in crow]
        at = [[at_ref[h, r, :] for h in heads] for r in crow]
        gate = [[gate_ref[0, r, hcols[h]] for h in heads] for r in crow]
        gts = [gct_ref[0, c] for c in cidx]
        st = [s_ref[h] for h in heads]
        ow = ow_ref[...]

        og = []
        for j in range(LOCAL_CHUNKS):
            eglast = [jnp.exp(gts[j][A_HEADS + h:A_HEADS + h + 1, CHUNK - 1:CHUNK]) for h in heads]
            ws_qs = [_mm(wq[j][h], st[h]) for h in heads]
            vnew = [u[j][h] - ws_qs[h][:CHUNK] for h in heads]
            o = [ws_qs[h][CHUNK:] + _mm(at[j][h], vnew[h]) for h in heads]
            st = [st[h] * eglast[h] + _mm_tn(kd[j][h], vnew[h]) for h in heads]
            og.append([_rms(o[h], ow) * _silu(gate[j][h]) for h in heads])

        def store():
            for h in heads:
                s_ref[h] = st[h]
                for j in range(LOCAL_CHUNKS):
                    o_ref[0, crow[j], hcols[h]] = og[j][h].astype(BF16)

        return store

    def fused(cc, carry):
        store_local = local(cc + 1)
        store_recur = recur(cc)
        store_local()
        store_recur()
        return carry

    groups = cb // LOCAL_CHUNKS
    local(0)()
    lax.fori_loop(0, groups - 1, fused, 0)
    recur(groups - 1)()


def _stage_delta(q, k, v, gate, bg, gct, onorm_w):
    bsz, L, _ = q.shape
    tm = _tile(L, DELTA_TILE)
    cb = tm // CHUNK
    row_spec = lambda w: pl.BlockSpec((1, tm, w), lambda b, i: (b, i, 0))
    return pl.pallas_call(
        functools.partial(_delta_kernel, cb=cb),
        grid=(bsz, L // tm),
        in_specs=[row_spec(A_WIDTH), row_spec(A_WIDTH), row_spec(A_WIDTH), row_spec(A_WIDTH),
                  row_spec(LANES),
                  pl.BlockSpec((1, cb, 2 * A_HEADS, CHUNK), lambda b, i: (b, i, 0, 0)),
                  _const_spec((1, A_HEAD_DIM))],
        out_specs=row_spec(A_WIDTH),
        out_shape=jax.ShapeDtypeStruct((bsz, L, A_WIDTH), BF16),
        scratch_shapes=[pltpu.VMEM((A_HEADS, A_HEAD_DIM, A_HEAD_DIM), F32),
                        pltpu.VMEM((tm, A_WIDTH), F32), pltpu.VMEM((tm, A_WIDTH), BF16),
                        pltpu.VMEM((tm, A_WIDTH), BF16), pltpu.VMEM((tm, A_WIDTH), BF16),
                        pltpu.VMEM((A_HEADS, tm, CHUNK), BF16)],
        compiler_params=_params(("arbitrary", "arbitrary")),
        name="delta_rule",
    )(q, k, v, gate, bg, gct, onorm_w.reshape(1, A_HEAD_DIM).astype(F32))


def _ffn_kernel(og_ref, x_ref, wo_ref, fw_ref, wg_ref, wu_ref, wd_ref, aw_ref, kw_ref, wq_ref,
                wkv_ref, h_ref, q_ref, kv_ref, *, d_ff):
    h1 = x_ref[...] + jnp.dot(og_ref[...], wo_ref[...], preferred_element_type=F32)
    hn = _rms(h1, fw_ref[...]).astype(BF16)
    acc = jnp.zeros_like(h1)
    for c in range(d_ff // FFN_COLS):
        cs = slice(c * FFN_COLS, (c + 1) * FFN_COLS)
        g = jnp.dot(hn, wg_ref[:, cs], preferred_element_type=F32)
        u = jnp.dot(hn, wu_ref[:, cs], preferred_element_type=F32)
        acc = acc + jnp.dot((_silu(g) * u).astype(BF16), wd_ref[cs, :],
                            preferred_element_type=F32)
    h2 = h1 + acc
    h_ref[...] = h2
    xh = h2 * lax.rsqrt(jnp.mean(h2 * h2, axis=-1, keepdims=True) + EPS)
    q = jnp.dot((xh * aw_ref[...]).astype(BF16), wq_ref[...], preferred_element_type=F32)
    q_ref[...] = (q * (B_HEAD_DIM ** -0.5)).astype(BF16)
    kv_ref[...] = jnp.dot((xh * kw_ref[...]).astype(BF16), wkv_ref[...],
                          preferred_element_type=F32).astype(BF16)


def _stage_ffn(og, x, w_out, ffn_norm_w, w_gate, w_up, w_down, attn_norm_w, kv_norm_w, w_q, w_kv):
    t, d = x.shape
    d_ff = w_gate.shape[1]
    tm = _tile(t, ROW_TILE)
    hq = B_Q_HEADS * B_HEAD_DIM
    half = B_KV_HEADS * B_HEAD_DIM
    dup = lambda w: jnp.concatenate([w.reshape(d, B_KV_HEADS, 1, B_HEAD_DIM)] * 2, axis=2).reshape(d, 2 * half)
    w_kvd = jnp.concatenate([dup(w_kv[:, :half]), dup(w_kv[:, half:])], axis=1).astype(BF16)
    row_spec = lambda w: pl.BlockSpec((tm, w), lambda i: (i, 0))
    return pl.pallas_call(
        functools.partial(_ffn_kernel, d_ff=d_ff),
        grid=(t // tm,),
        in_specs=[row_spec(A_WIDTH), row_spec(d), _const_spec((A_WIDTH, d)), _const_spec((1, d)),
                  _const_spec((d, d_ff)), _const_spec((d, d_ff)), _const_spec((d_ff, d)),
                  _const_spec((1, d)), _const_spec((1, d)), _const_spec((d, hq)),
                  _const_spec((d, 4 * half))],
        out_specs=[row_spec(d), row_spec(hq), row_spec(4 * half)],
        out_shape=[jax.ShapeDtypeStruct((t, d), F32), jax.ShapeDtypeStruct((t, hq), BF16),
                   jax.ShapeDtypeStruct((t, 4 * half), BF16)],
        compiler_params=_params(("arbitrary",)),
        name="ffn_dense",
    )(og, x, w_out.astype(BF16), ffn_norm_w.reshape(1, d), w_gate.astype(BF16), w_up.astype(BF16),
      w_down.astype(BF16), attn_norm_w.reshape(1, d), kv_norm_w.reshape(1, d), w_q.astype(BF16), w_kvd)


def _rel_buckets(rel):
    nb = REL_BUCKETS // 2
    max_exact = nb // 2
    ret = jnp.where(rel > 0, nb, 0)
    dist = jnp.abs(rel)
    dist_f = jnp.maximum(dist, 1).astype(F32)
    large = max_exact + (jnp.log(dist_f / max_exact) / math.log(REL_MAX_DIST / max_exact)
                         * (nb - max_exact)).astype(I32)
    large = jnp.minimum(large, nb - 1)
    return ret + jnp.where(dist < max_exact, dist, large)


def _bias_kernel(rb_ref, bk_ref, o_ref):
    bk = bk_ref[...]
    for h in range(B_Q_HEADS):
        acc = jnp.zeros(bk.shape, F32)
        for b in range(REL_BUCKETS):
            acc = jnp.where(bk == b, rb_ref[b * B_Q_HEADS + h], acc)
        o_ref[h] = acc


def _stage_bias(rel_bias):
    rel = (jnp.arange(BAND)[None, :] - WINDOW_CHUNKS * CHUNK) - jnp.arange(CHUNK)[:, None]
    buckets = _rel_buckets(rel).astype(I32)
    bias = pl.pallas_call(
        _bias_kernel,
        in_specs=[pl.BlockSpec(memory_space=pltpu.SMEM), pl.BlockSpec(memory_space=pltpu.VMEM)],
        out_specs=pl.BlockSpec(memory_space=pltpu.VMEM),
        out_shape=jax.ShapeDtypeStruct((B_Q_HEADS, CHUNK, BAND), F32),
        name="rel_bias",
    )(rel_bias.astype(F32).reshape(-1), buckets)
    pad = lambda lo, hi: jnp.pad(bias, ((0, 0), (0, 0), (lo, hi)), constant_values=-jnp.inf)
    pair = jnp.concatenate([pad(0, CHUNK), pad(CHUNK, 0)], axis=1)
    return pair.reshape(B_KV_HEADS, B_GROUP * ATT_Q, 2 * ATT_Q)


def _attn_kernel(q_ref, kv_ref, kvp_ref, bias_ref, sink_ref, wo_ref, h_ref, o_ref, att_ref, *, nblk):
    first = pl.program_id(1) == 0
    lane = lax.broadcasted_iota(I32, (ATT_Q, LANES), 1)
    lo_half = lane < B_HEAD_DIM
    kidx = lax.broadcasted_iota(I32, (B_GROUP * ATT_Q, 2 * ATT_Q), 1)
    koff = B_KV_HEADS * LANES
    for jb in range(nblk):
        rows = slice(jb * ATT_Q, (jb + 1) * ATT_Q)
        if jb == 0:
            keys = jnp.concatenate([kvp_ref[0], kv_ref[0, :ATT_Q, :]], axis=0)
        else:
            keys = kv_ref[0, (jb - 1) * ATT_Q:(jb + 1) * ATT_Q, :]
        for hk in range(B_KV_HEADS):
            kd = keys[:, hk * LANES:(hk + 1) * LANES]
            vd = keys[:, koff + hk * LANES:koff + (hk + 1) * LANES]
            qa = q_ref[0, rows, (2 * hk) * LANES:(2 * hk + 1) * LANES]
            qb = q_ref[0, rows, (2 * hk + 1) * LANES:(2 * hk + 2) * LANES]
            zero = jnp.zeros_like(qa)
            q4 = jnp.concatenate([jnp.where(lo_half, qa, zero), jnp.where(lo_half, zero, qa),
                                  jnp.where(lo_half, qb, zero), jnp.where(lo_half, zero, qb)], axis=0)
            s = lax.dot_general(q4, kd, (((1,), (1,)), ((), ())), preferred_element_type=F32)
            s = s + bias_ref[hk]
            if jb == 0:
                s = jnp.where(jnp.logical_and(first, kidx < ATT_Q), -jnp.inf, s)
            sink = sink_ref[hk]
            m = jnp.maximum(jnp.max(s, axis=-1, keepdims=True), sink)
            p = jnp.exp(s - m)
            denom = jnp.sum(p, axis=-1, keepdims=True) + jnp.exp(sink - m)
            o = jnp.dot(p.astype(BF16), vd, preferred_element_type=F32) / denom
            ta = jnp.where(lo_half, o[0:ATT_Q], o[ATT_Q:2 * ATT_Q])
            tb = jnp.where(lo_half, o[2 * ATT_Q:3 * ATT_Q], o[3 * ATT_Q:4 * ATT_Q])
            att_ref[rows, (2 * hk) * LANES:(2 * hk + 1) * LANES] = ta.astype(BF16)
            att_ref[rows, (2 * hk + 1) * LANES:(2 * hk + 2) * LANES] = tb.astype(BF16)
    o_ref[0] = h_ref[0] + jnp.dot(att_ref[...], wo_ref[...], preferred_element_type=F32)


def _stage_attn(q, kv, bias4, sinks, w_o, h):
    bsz, L, d = h.shape
    tq = _tile(L, ROW_TILE)
    halo = WINDOW_CHUNKS * CHUNK
    per = tq // halo
    hq = B_Q_HEADS * B_HEAD_DIM
    sink4 = jnp.repeat(sinks.astype(F32).reshape(B_KV_HEADS, B_GROUP), ATT_Q, axis=1)[..., None]
    row_spec = lambda w: pl.BlockSpec((1, tq, w), lambda b, i: (b, i, 0))
    return pl.pallas_call(
        functools.partial(_attn_kernel, nblk=tq // ATT_Q),
        grid=(bsz, L // tq),
        in_specs=[row_spec(hq), row_spec(kv.shape[-1]),
                  pl.BlockSpec((1, halo, kv.shape[-1]), lambda b, i: (b, jnp.maximum(i * per - 1, 0), 0)),
                  _const_spec(bias4.shape), _const_spec(sink4.shape), _const_spec((hq, d)),
                  row_spec(d)],
        out_specs=row_spec(d),
        out_shape=jax.ShapeDtypeStruct((bsz, L, d), F32),
        scratch_shapes=[pltpu.VMEM((tq, hq), BF16)],
        compiler_params=_params(("arbitrary", "arbitrary")),
        name="swa_attn",
    )(q.reshape(bsz, L, hq), kv.reshape(bsz, L, -1), kv.reshape(bsz, L, -1), bias4, sink4,
      w_o.astype(BF16), h)


def _router_kernel(h_ref, nw_ref, rw_ref, tri_ref, ei_ref, wcol_ref, cnt_ref, stat_ref, carry_ref, *, tr):
    @pl.when(pl.program_id(0) == 0)
    def _():
        carry_ref[...] = jnp.zeros_like(carry_ref)

    hn = _rms(h_ref[...], nw_ref[...])
    rows = 2 * N_EXPERTS
    lt = lax.dot_general(rw_ref[...], hn.astype(BF16), (((1,), (1,)), ((), ())),
                         preferred_element_type=F32)
    sub = lax.broadcasted_iota(I32, (rows, tr), 0)
    lt = jnp.where(sub < N_EXPERTS, lt, -jnp.inf)
    m1 = jnp.max(lt, axis=0, keepdims=True)
    i1 = jnp.min(jnp.where(lt == m1, sub, rows), axis=0, keepdims=True)
    lt2 = jnp.where(sub == i1, -jnp.inf, lt)
    m2 = jnp.max(lt2, axis=0, keepdims=True)
    i2 = jnp.min(jnp.where(lt2 == m2, sub, rows), axis=0, keepdims=True)
    e2 = jnp.exp(m2 - m1)
    w1 = 1.0 / (1.0 + e2)
    w2 = e2 / (1.0 + e2)
    hit = jnp.logical_or(sub == i1, sub == i2).astype(F32)
    pref = jnp.dot(hit.astype(BF16), tri_ref[...], preferred_element_type=F32)
    tile_cnt = jnp.sum(hit, axis=1, keepdims=True)
    seg_rows = jnp.floor((tile_cnt + (SEG_ALIGN - 1)) * (1.0 / SEG_ALIGN)) * SEG_ALIGN
    seg_rows = jnp.broadcast_to(seg_rows, (rows, LANES))
    below = jnp.zeros((rows, LANES), F32)
    sub_l = lax.broadcasted_iota(I32, (rows, LANES), 0)
    for e in range(N_EXPERTS - 1):
        below = below + jnp.where(sub_l > e, seg_rows[e:e + 1, :], 0.0)
    pos = pref + below[:, 0:1]
    r1 = jnp.sum(jnp.where(sub == i1, pos, 0.0), axis=0, keepdims=True)
    r2 = jnp.sum(jnp.where(sub == i2, pos, 0.0), axis=0, keepdims=True)
    before = carry_ref[...]
    lane = lax.broadcasted_iota(I32, (rows, LANES), 1)
    stat_ref[0] = jnp.where(lane == 0, seg_rows, before)
    total = before + seg_rows
    carry_ref[...] = total
    cnt_ref[...] = total
    ei = jnp.where(sub == 0, i1, jnp.where(sub == 1, i2, jnp.where(
        sub == 2, r1.astype(I32), jnp.where(sub == 3, r2.astype(I32), 0))))
    ei_ref[...] = ei[0:8]
    wrow = jnp.where(sub == 0, w1, jnp.where(sub == 1, w2, jnp.where(
        sub == 2, r1, jnp.where(sub == 3, r2, 0.0))))
    wpad = jnp.concatenate([wrow, jnp.zeros((LANES - rows, tr), F32)], axis=0)
    wcol_ref[...] = wpad.T


def _stage_router(h, norm_w, router_w):
    t, d = h.shape
    tr = _tile(t, ROW_TILE)
    rows = 2 * N_EXPERTS
    rwt = jnp.pad(router_w.T, ((0, rows - N_EXPERTS), (0, 0))).astype(BF16)
    r = np.arange(tr)
    tri = jnp.asarray((r[:, None] < r[None, :]).astype(np.float32)).astype(BF16)
    return pl.pallas_call(
        functools.partial(_router_kernel, tr=tr),
        grid=(t // tr,),
        in_specs=[pl.BlockSpec((tr, d), lambda i: (i, 0)), _const_spec((1, d)),
                  _const_spec((rows, d)), _const_spec((tr, tr))],
        out_specs=[pl.BlockSpec((8, tr), lambda i: (0, i)),
                   pl.BlockSpec((tr, LANES), lambda i: (i, 0)),
                   pl.BlockSpec((rows, LANES), lambda i: (0, 0)),
                   pl.BlockSpec((1, rows, LANES), lambda i: (i, 0, 0))],
        out_shape=[jax.ShapeDtypeStruct((8, t), I32),
                   jax.ShapeDtypeStruct((t, LANES), F32), jax.ShapeDtypeStruct((rows, LANES), F32),
                   jax.ShapeDtypeStruct((t // tr, rows, LANES), F32)],
        scratch_shapes=[pltpu.VMEM((rows, LANES), F32)],
        compiler_params=_params(("arbitrary",)),
        name="router",
    )(h, norm_w.reshape(1, d), rwt, tri)


def _sorted_rows(tile):
    return 2 * tile + LANES


def _segment_pieces(step, slen_ref, soff_ref, sdst_ref, rows):
    top = rows.bit_length() - 1
    for e in range(N_EXPERTS):
        n = slen_ref[step * N_EXPERTS + e]
        off = soff_ref[step * N_EXPERTS + e]
        dst = sdst_ref[step * N_EXPERTS + e]
        for k in range(top, SEG_ALIGN.bit_length() - 2, -1):
            done = (n >> (k + 1)) << (k + 1)
            yield ((n & (1 << k)) != 0, pl.multiple_of(off + done, SEG_ALIGN),
                   pl.multiple_of(dst + done, SEG_ALIGN), 1 << k)


def _dispatch_kernel(slen_ref, soff_ref, sdst_ref, h_ref, nw_ref, ei_ref, xs_in_ref, xs_ref,
                     srt_ref, sem, *, td):
    del xs_in_ref
    step = pl.program_id(0)
    hn = _rms(h_ref[...], nw_ref[...]).astype(BF16)
    slot = lax.broadcasted_iota(I32, (_sorted_rows(td), td), 0)
    pick = jnp.logical_or(slot == ei_ref[2:3, :], slot == ei_ref[3:4, :])
    srt_ref[...] = jnp.dot(jnp.where(pick, 1.0, 0.0).astype(BF16), hn, preferred_element_type=F32)

    def copy(src_row, dst_row, rows):
        return pltpu.make_async_copy(srt_ref.at[pl.ds(src_row, rows)], xs_ref.at[pl.ds(dst_row, rows)], sem)

    for present, src_row, dst_row, rows in _segment_pieces(step, slen_ref, soff_ref, sdst_ref, td):
        @pl.when(present)
        def _(src_row=src_row, dst_row=dst_row, rows=rows):
            copy(src_row, dst_row, rows).start()

    for present, src_row, dst_row, rows in _segment_pieces(step, slen_ref, soff_ref, sdst_ref, td):
        @pl.when(present)
        def _(src_row=src_row, dst_row=dst_row, rows=rows):
            copy(src_row, dst_row, rows).wait()


def _stage_dispatch(seg_len, seg_off, seg_dst, h, norm_w, ei, n_slots):
    t, d = h.shape
    td = _tile(t, ROW_TILE)
    xs0 = jnp.zeros((n_slots, d), F32)
    return pl.pallas_call(
        functools.partial(_dispatch_kernel, td=td),
        grid_spec=pltpu.PrefetchScalarGridSpec(
            num_scalar_prefetch=3, grid=(t // td,),
            in_specs=[pl.BlockSpec((td, d), lambda i, *_: (i, 0)),
                      pl.BlockSpec((1, d), lambda i, *_: (0, 0)),
                      pl.BlockSpec((8, td), lambda i, *_: (0, i)),
                      pl.BlockSpec(memory_space=pl.ANY)],
            out_specs=pl.BlockSpec(memory_space=pl.ANY),
            scratch_shapes=[pltpu.VMEM((_sorted_rows(td), d), F32), pltpu.SemaphoreType.DMA]),
        out_shape=jax.ShapeDtypeStruct((n_slots, d), F32),
        input_output_aliases={6: 0},
        compiler_params=pltpu.CompilerParams(dimension_semantics=("arbitrary",), has_side_effects=True,
                                             vmem_limit_bytes=VMEM_LIMIT),
        name="moe_dispatch",
    )(seg_len, seg_off, seg_dst, h, norm_w.reshape(1, d), ei, xs0)


def _moe_kernel(be_ref, nu_ref, xs_ref, wg_ref, wu_ref, wd_ref, ys_ref, xb_ref, hd_ref, *, tf):
    del be_ref
    f = pl.program_id(1)

    @pl.when(pl.program_id(0) < nu_ref[0])
    def _():
        @pl.when(f == 0)
        def _():
            xb_ref[...] = xs_ref[...].astype(BF16)

        x = xb_ref[...]
        for c in range(tf // MOE_SUB):
            cs = slice(c * MOE_SUB, (c + 1) * MOE_SUB)
            g = jnp.dot(x, wg_ref[0, :, cs], preferred_element_type=F32)
            u = jnp.dot(x, wu_ref[0, :, cs], preferred_element_type=F32)
            hd_ref[:, cs] = (_silu(g) * u).astype(BF16)
        y = jnp.dot(hd_ref[...], wd_ref[0], preferred_element_type=F32)

        @pl.when(f == 0)
        def _():
            ys_ref[...] = y

        @pl.when(f > 0)
        def _():
            ys_ref[...] += y

    @pl.when(jnp.logical_and(pl.program_id(0) >= nu_ref[0], f == 0))
    def _():
        ys_ref[...] = jnp.zeros_like(ys_ref)


def _stage_moe(blk_e, n_used, xs, w_gate, w_up, w_down):
    n_slots, d = xs.shape
    d_ff = w_gate.shape[-1]
    tm = MOE_ROWS
    tf = _tile(d_ff, MOE_COLS)
    nb, nf = n_slots // tm, d_ff // tf

    def blk(b, nu):
        return jnp.minimum(b, nu[0] - 1)

    def col(b, f, nu):
        return jnp.where(b < nu[0], f, nf - 1)

    return pl.pallas_call(
        functools.partial(_moe_kernel, tf=tf),
        grid_spec=pltpu.PrefetchScalarGridSpec(
            num_scalar_prefetch=2, grid=(nb, nf),
            in_specs=[pl.BlockSpec((tm, d), lambda b, f, be, nu: (blk(b, nu), 0)),
                      pl.BlockSpec((1, d, tf), lambda b, f, be, nu: (be[blk(b, nu)], 0, col(b, f, nu))),
                      pl.BlockSpec((1, d, tf), lambda b, f, be, nu: (be[blk(b, nu)], 0, col(b, f, nu))),
                      pl.BlockSpec((1, tf, d), lambda b, f, be, nu: (be[blk(b, nu)], col(b, f, nu), 0))],
            out_specs=pl.BlockSpec((tm, d), lambda b, f, be, nu: (b, 0)),
            scratch_shapes=[pltpu.VMEM((tm, d), BF16), pltpu.VMEM((tm, tf), BF16)]),
        out_shape=jax.ShapeDtypeStruct((n_slots, d), F32),
        compiler_params=_params(("arbitrary", "arbitrary")),
        name="moe_experts",
    )(blk_e, n_used, xs, w_gate.astype(BF16), w_up.astype(BF16), w_down.astype(BF16))


def _combine_kernel(slen_ref, soff_ref, sdst_ref, h_ref, w_ref, nw_ref, ys_ref, o_ref, buf_ref, sem,
                    *, tc):
    i = pl.program_id(0)
    n = pl.num_programs(0)

    def copy(step_slot, src_row, dst_row, rows):
        return pltpu.make_async_copy(ys_ref.at[pl.ds(src_row, rows)],
                                     buf_ref.at[step_slot, pl.ds(dst_row, rows)], sem.at[step_slot])

    def gather(step, step_slot):
        for present, off, dst, rows in _segment_pieces(step, slen_ref, soff_ref, sdst_ref, tc):
            @pl.when(present)
            def _(off=off, dst=dst, rows=rows):
                copy(step_slot, dst, off, rows).start()

    @pl.when(i == 0)
    def _():
        buf_ref[:, 2 * tc:, :] = jnp.zeros((2, _sorted_rows(tc) - 2 * tc, buf_ref.shape[-1]), F32)
        gather(0, 0)

    slot = lax.rem(i, 2)

    @pl.when(i + 1 < n)
    def _():
        gather(i + 1, 1 - slot)

    for present, off, dst, rows in _segment_pieces(i, slen_ref, soff_ref, sdst_ref, tc):
        @pl.when(present)
        def _(off=off, dst=dst, rows=rows):
            copy(slot, dst, off, rows).wait()

    yb = buf_ref[slot].astype(BF16)
    pos = lax.broadcasted_iota(I32, (tc, _sorted_rows(tc)), 1)
    sel0 = jnp.where(pos == w_ref[:, 2:3].astype(I32), 1.0, 0.0).astype(BF16)
    sel1 = jnp.where(pos == w_ref[:, 3:4].astype(I32), 1.0, 0.0).astype(BF16)
    y0 = jnp.dot(sel0, yb, preferred_element_type=F32)
    y1 = jnp.dot(sel1, yb, preferred_element_type=F32)
    h = h_ref[...] + (y0 * w_ref[:, 0:1] + y1 * w_ref[:, 1:2])
    o_ref[...] = _rms(h, nw_ref[...])


def _stage_combine(seg_len, seg_off, seg_dst, h, wcol, norm_w, ys):
    t, d = h.shape
    tc = _tile(t, ROW_TILE)
    return pl.pallas_call(
        functools.partial(_combine_kernel, tc=tc),
        grid_spec=pltpu.PrefetchScalarGridSpec(
            num_scalar_prefetch=3, grid=(t // tc,),
            in_specs=[pl.BlockSpec((tc, d), lambda i, *_: (i, 0)),
                      pl.BlockSpec((tc, LANES), lambda i, *_: (i, 0)),
                      pl.BlockSpec((1, d), lambda i, *_: (0, 0)),
                      pl.BlockSpec(memory_space=pl.ANY)],
            out_specs=pl.BlockSpec((tc, d), lambda i, *_: (i, 0)),
            scratch_shapes=[pltpu.VMEM((2, _sorted_rows(tc), d), F32), pltpu.SemaphoreType.DMA((2,))]),
        out_shape=jax.ShapeDtypeStruct((t, d), F32),
        compiler_params=_params(("arbitrary",)),
        name="moe_combine",
    )(seg_len, seg_off, seg_dst, h, wcol, norm_w.reshape(1, d), ys)


def kernel(x, attn_norm_w, ffn_norm_w, a_w_in, a_conv_w, a_a_log, a_dt_bias, a_onorm_w, a_w_out,
           kv_norm_w, w_kv, b_w_q, b_sinks, b_w_o, rel_bias, ffn_w_gate, ffn_w_up, ffn_w_down,
           moe_router, moe_w_gate, moe_w_up, moe_w_down, final_norm_w):
    bsz, L, d = x.shape
    t = bsz * L

    q, k, v, gate, bg, gct = _stage_in(x, attn_norm_w[0], a_w_in[0], a_conv_w[0], a_a_log[0],
                                       a_dt_bias[0])
    og = _stage_delta(q, k, v, gate, bg, gct, a_onorm_w[0])
    h, q2, kv2 = _stage_ffn(og.reshape(t, A_WIDTH), x.reshape(t, d), a_w_out[0], ffn_norm_w[0],
                            ffn_w_gate[0], ffn_w_up[0], ffn_w_down[0], attn_norm_w[1], kv_norm_w,
                            b_w_q[0], w_kv)

    bias4 = _stage_bias(rel_bias)
    h = _stage_attn(q2, kv2, bias4, b_sinks[0], b_w_o[0], h.reshape(bsz, L, d)).reshape(t, d)

    ei, wcol, cnt, stat = _stage_router(h, ffn_norm_w[1], moe_router[0])
    counts = cnt[:N_EXPERTS, 0].astype(I32)
    p_counts = (counts + MOE_ROWS - 1) // MOE_ROWS * MOE_ROWS
    p_end = jnp.cumsum(p_counts)
    p_start = p_end - p_counts
    tile_cnt = stat[:, :N_EXPERTS, 0].astype(I32)
    before = stat[:, :N_EXPERTS, 1].astype(I32)
    seg_len = tile_cnt.reshape(-1)
    seg_off = (jnp.cumsum(tile_cnt, axis=1) - tile_cnt).reshape(-1)
    seg_dst = (p_start[None, :] + before).reshape(-1)
    n_tiles = stat.shape[0]
    n_blocks = -(-(2 * t + n_tiles * N_EXPERTS * (SEG_ALIGN - 1)) // MOE_ROWS) + N_EXPERTS
    blk_e = jnp.minimum(jnp.searchsorted(p_end, jnp.arange(n_blocks, dtype=I32) * MOE_ROWS, side='right'),
                        N_EXPERTS - 1).astype(I32)
    n_used = (p_end[-1:] // MOE_ROWS).astype(I32)
    xs = _stage_dispatch(seg_len, seg_off, seg_dst, h, ffn_norm_w[1], ei, n_blocks * MOE_ROWS)
    ys = _stage_moe(blk_e, n_used, xs, moe_w_gate[0], moe_w_up[0], moe_w_down[0])
    out = _stage_combine(seg_len, seg_off, seg_dst, h, wcol, final_norm_w, ys)
    return out.reshape(bsz, L, d)
```

```python
import functools
import math

import numpy as np
import jax
import jax.numpy as jnp
from jax import lax
from jax.experimental import pallas as pl
from jax.experimental.pallas import tpu as pltpu

F32 = jnp.float32
BF16 = jnp.bfloat16
I32 = jnp.int32

EPS = 1e-6
CHUNK = 64
A_HEADS = 8
A_HEAD_DIM = 128
A_WIDTH = A_HEADS * A_HEAD_DIM
A_CONV = 4
B_Q_HEADS = 16
B_KV_HEADS = 4
B_HEAD_DIM = 64
B_GROUP = B_Q_HEADS // B_KV_HEADS
WINDOW_CHUNKS = 2
BAND = (WINDOW_CHUNKS + 1) * CHUNK
REL_BUCKETS = 32
REL_MAX_DIST = 128
N_EXPERTS = 8
LANES = 128

VMEM_LIMIT = 56 * 1024 * 1024

DELTA_TILE = 1024
ROW_TILE = 512
CONV_COLS = 512
FFN_COLS = 256
MOE_ROWS = 1024
MOE_COLS = 1792
SEG_ALIGN = 8
MOE_SUB = 256
ATT_Q = 2 * CHUNK
LOCAL_CHUNKS = 2


def _tile(n, pref):
    t = min(n, pref)
    assert n % t == 0, (n, t)
    return t


def _mm(a, b):
    return jnp.dot(a.astype(BF16), b.astype(BF16), preferred_element_type=F32)


def _mm_nt(a, b):
    return lax.dot_general(a.astype(BF16), b.astype(BF16), (((1,), (1,)), ((), ())),
                           preferred_element_type=F32)


def _mm_tn(a, b):
    return lax.dot_general(a.astype(BF16), b.astype(BF16), (((0,), (0,)), ((), ())),
                           preferred_element_type=F32)


def _silu(x):
    return x * jax.nn.sigmoid(x)


def _rms(x, w):
    return x * lax.rsqrt(jnp.mean(x * x, axis=-1, keepdims=True) + EPS) * w


def _const_spec(shape):
    nd = len(shape)
    return pl.BlockSpec(shape, lambda *_: (0,) * nd, pipeline_mode=pl.Buffered(1))


def _params(sem):
    return pltpu.CompilerParams(dimension_semantics=sem, vmem_limit_bytes=VMEM_LIMIT)


def _in_kernel(x_ref, nw_ref, w_ref, wba_ref, cw_ref, ap_ref, seg_ref,
               q_ref, k_ref, v_ref, gate_ref, bg_ref, gct_ref, ext_ref, *, tm):
    @pl.when(pl.program_id(1) == 0)
    def _():
        ext_ref[0:8, :] = jnp.zeros((8, 3 * A_WIDTH), F32)

    hn = _rms(x_ref[0], nw_ref[...]).astype(BF16)
    outs = (q_ref, k_ref, v_ref)
    for c in range(3 * A_WIDTH // CONV_COLS):
        cs = slice(c * CONV_COLS, (c + 1) * CONV_COLS)
        p = jnp.dot(hn, w_ref[:, cs], preferred_element_type=F32)
        ext_ref[8:tm + 8, cs] = p
        acc = cw_ref[A_CONV - 1:A_CONV, cs] * p
        for j in range(A_CONV - 1):
            acc = acc + cw_ref[j:j + 1, cs] * ext_ref[5 + j:5 + j + tm, cs]
        ext_ref[0:8, cs] = p[tm - 8:, :]
        a = _silu(acc)
        which, off = divmod(c * CONV_COLS, A_WIDTH)
        for hh in range(CONV_COLS // A_HEAD_DIM):
            hs = a[:, hh * A_HEAD_DIM:(hh + 1) * A_HEAD_DIM]
            if which < 2:
                hs = hs * lax.rsqrt(jnp.sum(hs * hs, axis=-1, keepdims=True) + EPS)
            if which == 0:
                hs = hs * (A_HEAD_DIM ** -0.5)
            lo = off + hh * A_HEAD_DIM
            outs[which][0, :, lo:lo + A_HEAD_DIM] = hs.astype(BF16)

    for c in range(A_WIDTH // CONV_COLS):
        cs = slice(3 * A_WIDTH + c * CONV_COLS, 3 * A_WIDTH + (c + 1) * CONV_COLS)
        gate_ref[0, :, c * CONV_COLS:(c + 1) * CONV_COLS] = jnp.dot(
            hn, w_ref[:, cs], preferred_element_type=F32)

    ba = jnp.dot(hn, wba_ref[...], preferred_element_type=F32)
    beta = jax.nn.sigmoid(ba)
    z = ba + ap_ref[1:2, :]
    softplus = jnp.maximum(z, 0.0) + jnp.log1p(jnp.exp(-jnp.abs(z)))
    g = -jnp.exp(ap_ref[0:1, :]) * softplus
    g1 = g.astype(BF16)
    r1 = g - g1.astype(F32)
    g2 = r1.astype(BF16)
    g3 = (r1 - g2.astype(F32)).astype(BF16)
    seg = seg_ref[...]
    gc = (jnp.dot(seg, g1, preferred_element_type=F32) + jnp.dot(seg, g2, preferred_element_type=F32)
          + jnp.dot(seg, g3, preferred_element_type=F32))
    lane = lax.broadcasted_iota(I32, (tm, LANES), 1)
    bg = jnp.where(lane < A_HEADS, beta, gc)
    bg_ref[0] = bg
    bgt = bg.T
    for ci in range(tm // CHUNK):
        gct_ref[0, ci] = bgt[0:2 * A_HEADS, ci * CHUNK:(ci + 1) * CHUNK]


def _stage_in(x, norm_w, w_in, conv_w, a_log, dt_bias):
    bsz, L, d = x.shape
    tm = _tile(L, ROW_TILE)
    nc = L // CHUNK
    w_main = w_in[:, :4 * A_WIDTH].astype(BF16)
    w_ba = jnp.pad(w_in[:, 4 * A_WIDTH:], ((0, 0), (0, LANES - 2 * A_HEADS))).astype(BF16)
    ap = jnp.zeros((2, LANES), F32)
    ap = ap.at[0, A_HEADS:2 * A_HEADS].set(a_log.astype(F32))
    ap = ap.at[1, A_HEADS:2 * A_HEADS].set(dt_bias.astype(F32))
    r = np.arange(tm)
    seg = jnp.asarray(((r[:, None] >= r[None, :]) &
                       (r[:, None] // CHUNK == r[None, :] // CHUNK)).astype(np.float32)).astype(BF16)
    row_spec = lambda w: pl.BlockSpec((1, tm, w), lambda b, i: (b, i, 0))
    act = jax.ShapeDtypeStruct((bsz, L, A_WIDTH), BF16)
    return pl.pallas_call(
        functools.partial(_in_kernel, tm=tm),
        grid=(bsz, L // tm),
        in_specs=[row_spec(d), _const_spec((1, d)), _const_spec((d, 4 * A_WIDTH)),
                  _const_spec((d, LANES)), _const_spec((A_CONV, 3 * A_WIDTH)),
                  _const_spec((2, LANES)), _const_spec((tm, tm))],
        out_specs=[row_spec(A_WIDTH), row_spec(A_WIDTH), row_spec(A_WIDTH), row_spec(A_WIDTH),
                   row_spec(LANES),
                   pl.BlockSpec((1, tm // CHUNK, 2 * A_HEADS, CHUNK), lambda b, i: (b, i, 0, 0))],
        out_shape=[act, act, act, jax.ShapeDtypeStruct((bsz, L, A_WIDTH), F32),
                   jax.ShapeDtypeStruct((bsz, L, LANES), F32),
                   jax.ShapeDtypeStruct((bsz, nc, 2 * A_HEADS, CHUNK), F32)],
        scratch_shapes=[pltpu.VMEM((tm + 8, 3 * A_WIDTH), F32)],
        compiler_params=_params(("arbitrary", "arbitrary")),
        name="in_proj",
    )(x, norm_w.reshape(1, d), w_main, w_ba, conv_w.astype(F32), ap, seg)


def _chunk_rows(c):
    if isinstance(c, int):
        return slice(c * CHUNK, (c + 1) * CHUNK)
    return pl.ds(pl.multiple_of(c * CHUNK, CHUNK), CHUNK)


def _delta_kernel(q_ref, k_ref, v_ref, gate_ref, bg_ref, gct_ref, ow_ref, o_ref,
                  s_ref, u_ref, w_ref, qd_ref, kd_ref, at_ref, *, cb):
    @pl.when(pl.program_id(1) == 0)
    def _():
        s_ref[...] = jnp.zeros_like(s_ref)

    heads = range(A_HEADS)
    hcols = [slice(h * A_HEAD_DIM, (h + 1) * A_HEAD_DIM) for h in heads]
    row = lax.broadcasted_iota(I32, (CHUNK, CHUNK), 0)
    col = lax.broadcasted_iota(I32, (CHUNK, CHUNK), 1)
    tri = row >= col
    strict = row > col
    diag = (row >> 4) == (col >> 4)
    eye = (row == col).astype(F32)

    def local(cc):
        items = [(j, h) for j in range(LOCAL_CHUNKS) for h in heads]
        its = range(len(items))
        cidx = [cc * LOCAL_CHUNKS + j for j in range(LOCAL_CHUNKS)]
        crow = [_chunk_rows(c) for c in cidx]
        rows = [crow[j] for j, _ in items]
        cols = [hcols[h] for _, h in items]
        qb = [q_ref[0, rows[i], cols[i]] for i in its]
        kb = [k_ref[0, rows[i], cols[i]] for i in its]
        vb = [v_ref[0, rows[i], cols[i]] for i in its]
        bgs = [bg_ref[0, crow[j], :] for j in range(LOCAL_CHUNKS)]
        gts = [gct_ref[0, c] for c in cidx]
        beta = [bgs[j][:, h:h + 1] for j, h in items]
        gcol = [bgs[j][:, A_HEADS + h:A_HEADS + h + 1] for j, h in items]
        grow = [gts[j][A_HEADS + h:A_HEADS + h + 1, :] for j, h in items]

        decay = [jnp.exp(jnp.where(tri, gcol[i] - grow[i], -jnp.inf)) for i in its]
        eg = [jnp.exp(gcol[i]) for i in its]
        kf = [kb[i].astype(F32) for i in its]
        kbeta = [kf[i] * beta[i] for i in its]
        lower = [jnp.where(strict, _mm_nt(kbeta[i], kb[i]) * decay[i], 0.0) for i in its]
        ld = [jnp.where(diag, lower[i], 0.0) for i in its]
        lo = [lower[i] - ld[i] for i in its]
        p = [eye - ld[i] for i in its]
        s = [_mm(ld[i], ld[i]) for i in its]
        for _ in range(2):
            p = [p[i] + _mm(p[i], s[i]) for i in its]
            s = [_mm(s[i], s[i]) for i in its]
        td = [p[i] + _mm(p[i], s[i]) for i in its]
        rhs = [jnp.concatenate([vb[i].astype(F32) * beta[i], kbeta[i] * eg[i]], axis=1) for i in its]
        r = [_mm(td[i], rhs[i]) for i in its]
        m = [_mm(td[i], lo[i]) for i in its]
        m2 = [_mm(m[i], m[i]) for i in its]
        r = [r[i] + _mm(m2[i], r[i]) for i in its]
        uw = [r[i] - _mm(m[i], r[i]) for i in its]
        attn = [_mm_nt(qb[i], kb[i]) * decay[i] for i in its]
        qd = [qb[i].astype(F32) * eg[i] for i in its]
        kd = [kf[i] * jnp.exp(grow[i][:, CHUNK - 1:CHUNK] - gcol[i]) for i in its]

        def store():
            for i, (_, h) in enumerate(items):
                u_ref[rows[i], cols[i]] = uw[i][:, :A_HEAD_DIM]
                w_ref[rows[i], cols[i]] = uw[i][:, A_HEAD_DIM:].astype(BF16)
                qd_ref[rows[i], cols[i]] = qd[i].astype(BF16)
                kd_ref[rows[i], cols[i]] = kd[i].astype(BF16)
                at_ref[h, rows[i], :] = attn[i].astype(BF16)

        return store

    def recur(cc):
        cidx = [cc * LOCAL_CHUNKS + j for j in range(LOCAL_CHUNKS)]
        crow = [_chunk_rows(c) for c in cidx]
        wq = [[jnp.concatenate([w_ref[r, hcols[h]], qd_ref[r, hcols[h]]], axis=0) for h in heads]
              for r in crow]
        u = [[u_ref[r, hcols[h]] for h in heads] for r in crow]
        kd = [[kd_ref[r, hcols[h]] for h in heads] for r in crow]
        at = [[at_ref[h, r, :] for h in heads] for r in crow]
        gate = [[gate_ref[0, r, hcols[h]] for h in heads] for r in crow]
        gts = [gct_ref[0, c] for c in cidx]
        st = [s_ref[h] for h in heads]
        ow = ow_ref[...]

        og = []
        for j in range(LOCAL_CHUNKS):
            eglast = [jnp.exp(gts[j][A_HEADS + h:A_HEADS + h + 1, CHUNK - 1:CHUNK]) for h in heads]
            ws_qs = [_mm(wq[j][h], st[h]) for h in heads]
            vnew = [u[j][h] - ws_qs[h][:CHUNK] for h in heads]
            o = [ws_qs[h][CHUNK:] + _mm(at[j][h], vnew[h]) for h in heads]
            st = [st[h] * eglast[h] + _mm_tn(kd[j][h], vnew[h]) for h in heads]
            og.append([_rms(o[h], ow) * _silu(gate[j][h]) for h in heads])

        def store():
            for h in heads:
                s_ref[h] = st[h]
                for j in range(LOCAL_CHUNKS):
                    o_ref[0, crow[j], hcols[h]] = og[j][h].astype(BF16)

        return store

    def fused(cc, carry):
        store_local = local(cc + 1)
        store_recur = recur(cc)
        store_local()
        store_recur()
        return carry

    groups = cb // LOCAL_CHUNKS
    local(0)()
    lax.fori_loop(0, groups - 1, fused, 0)
    recur(groups - 1)()


def _stage_delta(q, k, v, gate, bg, gct, onorm_w):
    bsz, L, _ = q.shape
    tm = _tile(L, DELTA_TILE)
    cb = tm // CHUNK
    row_spec = lambda w: pl.BlockSpec((1, tm, w), lambda b, i: (b, i, 0))
    return pl.pallas_call(
        functools.partial(_delta_kernel, cb=cb),
        grid=(bsz, L // tm),
        in_specs=[row_spec(A_WIDTH), row_spec(A_WIDTH), row_spec(A_WIDTH), row_spec(A_WIDTH),
                  row_spec(LANES),
                  pl.BlockSpec((1, cb, 2 * A_HEADS, CHUNK), lambda b, i: (b, i, 0, 0)),
                  _const_spec((1, A_HEAD_DIM))],
        out_specs=row_spec(A_WIDTH),
        out_shape=jax.ShapeDtypeStruct((bsz, L, A_WIDTH), BF16),
        scratch_shapes=[pltpu.VMEM((A_HEADS, A_HEAD_DIM, A_HEAD_DIM), F32),
                        pltpu.VMEM((tm, A_WIDTH), F32), pltpu.VMEM((tm, A_WIDTH), BF16),
                        pltpu.VMEM((tm, A_WIDTH), BF16), pltpu.VMEM((tm, A_WIDTH), BF16),
                        pltpu.VMEM((A_HEADS, tm, CHUNK), BF16)],
        compiler_params=_params(("arbitrary", "arbitrary")),
        name="delta_rule",
    )(q, k, v, gate, bg, gct, onorm_w.reshape(1, A_HEAD_DIM).astype(F32))


def _ffn_kernel(og_ref, x_ref, wo_ref, fw_ref, wg_ref, wu_ref, wd_ref, aw_ref, kw_ref, wq_ref,
                wkv_ref, h_ref, q_ref, kv_ref, *, d_ff):
    h1 = x_ref[...] + jnp.dot(og_ref[...], wo_ref[...], preferred_element_type=F32)
    hn = _rms(h1, fw_ref[...]).astype(BF16)
    acc = jnp.zeros_like(h1)
    for c in range(d_ff // FFN_COLS):
        cs = slice(c * FFN_COLS, (c + 1) * FFN_COLS)
        g = jnp.dot(hn, wg_ref[:, cs], preferred_element_type=F32)
        u = jnp.dot(hn, wu_ref[:, cs], preferred_element_type=F32)
        acc = acc + jnp.dot((_silu(g) * u).astype(BF16), wd_ref[cs, :],
                            preferred_element_type=F32)
    h2 = h1 + acc
    h_ref[...] = h2
    xh = h2 * lax.rsqrt(jnp.mean(h2 * h2, axis=-1, keepdims=True) + EPS)
    q = jnp.dot((xh * aw_ref[...]).astype(BF16), wq_ref[...], preferred_element_type=F32)
    q_ref[...] = (q * (B_HEAD_DIM ** -0.5)).astype(BF16)
    kv_ref[...] = jnp.dot((xh * kw_ref[...]).astype(BF16), wkv_ref[...],
                          preferred_element_type=F32).astype(BF16)


def _stage_ffn(og, x, w_out, ffn_norm_w, w_gate, w_up, w_down, attn_norm_w, kv_norm_w, w_q, w_kv):
    t, d = x.shape
    d_ff = w_gate.shape[1]
    tm = _tile(t, ROW_TILE)
    hq = B_Q_HEADS * B_HEAD_DIM
    half = B_KV_HEADS * B_HEAD_DIM
    dup = lambda w: jnp.concatenate([w.reshape(d, B_KV_HEADS, 1, B_HEAD_DIM)] * 2, axis=2).reshape(d, 2 * half)
    w_kvd = jnp.concatenate([dup(w_kv[:, :half]), dup(w_kv[:, half:])], axis=1).astype(BF16)
    row_spec = lambda w: pl.BlockSpec((tm, w), lambda i: (i, 0))
    return pl.pallas_call(
        functools.partial(_ffn_kernel, d_ff=d_ff),
        grid=(t // tm,),
        in_specs=[row_spec(A_WIDTH), row_spec(d), _const_spec((A_WIDTH, d)), _const_spec((1, d)),
                  _const_spec((d, d_ff)), _const_spec((d, d_ff)), _const_spec((d_ff, d)),
                  _const_spec((1, d)), _const_spec((1, d)), _const_spec((d, hq)),
                  _const_spec((d, 4 * half))],
        out_specs=[row_spec(d), row_spec(hq), row_spec(4 * half)],
        out_shape=[jax.ShapeDtypeStruct((t, d), F32), jax.ShapeDtypeStruct((t, hq), BF16),
                   jax.ShapeDtypeStruct((t, 4 * half), BF16)],
        compiler_params=_params(("arbitrary",)),
        name="ffn_dense",
    )(og, x, w_out.astype(BF16), ffn_norm_w.reshape(1, d), w_gate.astype(BF16), w_up.astype(BF16),
      w_down.astype(BF16), attn_norm_w.reshape(1, d), kv_norm_w.reshape(1, d), w_q.astype(BF16), w_kvd)


def _rel_buckets(rel):
    nb = REL_BUCKETS // 2
    max_exact = nb // 2
    ret = jnp.where(rel > 0, nb, 0)
    dist = jnp.abs(rel)
    dist_f = jnp.maximum(dist, 1).astype(F32)
    large = max_exact + (jnp.log(dist_f / max_exact) / math.log(REL_MAX_DIST / max_exact)
                         * (nb - max_exact)).astype(I32)
    large = jnp.minimum(large, nb - 1)
    return ret + jnp.where(dist < max_exact, dist, large)


def _bias_kernel(rb_ref, bk_ref, o_ref):
    bk = bk_ref[...]
    for h in range(B_Q_HEADS):
        acc = jnp.zeros(bk.shape, F32)
        for b in range(REL_BUCKETS):
            acc = jnp.where(bk == b, rb_ref[b * B_Q_HEADS + h], acc)
        o_ref[h] = acc


def _stage_bias(rel_bias):
    rel = (jnp.arange(BAND)[None, :] - WINDOW_CHUNKS * CHUNK) - jnp.arange(CHUNK)[:, None]
    buckets = _rel_buckets(rel).astype(I32)
    bias = pl.pallas_call(
        _bias_kernel,
        in_specs=[pl.BlockSpec(memory_space=pltpu.SMEM), pl.BlockSpec(memory_space=pltpu.VMEM)],
        out_specs=pl.BlockSpec(memory_space=pltpu.VMEM),
        out_shape=jax.ShapeDtypeStruct((B_Q_HEADS, CHUNK, BAND), F32),
        name="rel_bias",
    )(rel_bias.astype(F32).reshape(-1), buckets)
    pad = lambda lo, hi: jnp.pad(bias, ((0, 0), (0, 0), (lo, hi)), constant_values=-jnp.inf)
    pair = jnp.concatenate([pad(0, CHUNK), pad(CHUNK, 0)], axis=1)
    return pair.reshape(B_KV_HEADS, B_GROUP * ATT_Q, 2 * ATT_Q)


def _attn_kernel(q_ref, kv_ref, kvp_ref, bias_ref, sink_ref, wo_ref, h_ref, nw_ref, rw_ref, tri_ref,
                 o_ref, ei_ref, wcol_ref, cnt_ref, stat_ref, att_ref, carry_ref, *, nblk, tq):
    first = pl.program_id(1) == 0
    lane = lax.broadcasted_iota(I32, (ATT_Q, LANES), 1)
    lo_half = lane < B_HEAD_DIM
    kidx = lax.broadcasted_iota(I32, (B_GROUP * ATT_Q, 2 * ATT_Q), 1)
    koff = B_KV_HEADS * LANES
    for jb in range(nblk):
        rows = slice(jb * ATT_Q, (jb + 1) * ATT_Q)
        if jb == 0:
            keys = jnp.concatenate([kvp_ref[0], kv_ref[0, :ATT_Q, :]], axis=0)
        else:
            keys = kv_ref[0, (jb - 1) * ATT_Q:(jb + 1) * ATT_Q, :]
        for hk in range(B_KV_HEADS):
            kd = keys[:, hk * LANES:(hk + 1) * LANES]
            vd = keys[:, koff + hk * LANES:koff + (hk + 1) * LANES]
            qa = q_ref[0, rows, (2 * hk) * LANES:(2 * hk + 1) * LANES]
            qb = q_ref[0, rows, (2 * hk + 1) * LANES:(2 * hk + 2) * LANES]
            zero = jnp.zeros_like(qa)
            q4 = jnp.concatenate([jnp.where(lo_half, qa, zero), jnp.where(lo_half, zero, qa),
                                  jnp.where(lo_half, qb, zero), jnp.where(lo_half, zero, qb)], axis=0)
            s = lax.dot_general(q4, kd, (((1,), (1,)), ((), ())), preferred_element_type=F32)
            s = s + bias_ref[hk]
            if jb == 0:
                s = jnp.where(jnp.logical_and(first, kidx < ATT_Q), -jnp.inf, s)
            sink = sink_ref[hk]
            m = jnp.maximum(jnp.max(s, axis=-1, keepdims=True), sink)
            p = jnp.exp(s - m)
            denom = jnp.sum(p, axis=-1, keepdims=True) + jnp.exp(sink - m)
            o = jnp.dot(p.astype(BF16), vd, preferred_element_type=F32) / denom
            ta = jnp.where(lo_half, o[0:ATT_Q], o[ATT_Q:2 * ATT_Q])
            tb = jnp.where(lo_half, o[2 * ATT_Q:3 * ATT_Q], o[3 * ATT_Q:4 * ATT_Q])
            att_ref[rows, (2 * hk) * LANES:(2 * hk + 1) * LANES] = ta.astype(BF16)
            att_ref[rows, (2 * hk + 1) * LANES:(2 * hk + 2) * LANES] = tb.astype(BF16)
    h_out = h_ref[0] + jnp.dot(att_ref[...], wo_ref[...], preferred_element_type=F32)
    o_ref[0] = h_out
    _route(h_out, jnp.logical_and(pl.program_id(0) == 0, first), nw_ref, rw_ref, tri_ref,
           ei_ref, wcol_ref, cnt_ref, stat_ref, carry_ref, tq)


def _stage_attn(q, kv, bias4, sinks, w_o, h, ffn_norm_w, router_w):
    bsz, L, d = h.shape
    tq = _tile(L, ROW_TILE)
    nt = L // tq
    halo = WINDOW_CHUNKS * CHUNK
    per = tq // halo
    hq = B_Q_HEADS * B_HEAD_DIM
    sink4 = jnp.repeat(sinks.astype(F32).reshape(B_KV_HEADS, B_GROUP), ATT_Q, axis=1)[..., None]
    rows = 2 * N_EXPERTS
    rwt = jnp.pad(router_w.T, ((0, rows - N_EXPERTS), (0, 0))).astype(BF16)
    r = np.arange(tq)
    tri = jnp.asarray((r[:, None] < r[None, :]).astype(np.float32)).astype(BF16)
    row_spec = lambda w: pl.BlockSpec((1, tq, w), lambda b, i: (b, i, 0))
    return pl.pallas_call(
        functools.partial(_attn_kernel, nblk=tq // ATT_Q, tq=tq),
        grid=(bsz, nt),
        in_specs=[row_spec(hq), row_spec(kv.shape[-1]),
                  pl.BlockSpec((1, halo, kv.shape[-1]), lambda b, i: (b, jnp.maximum(i * per - 1, 0), 0)),
                  _const_spec(bias4.shape), _const_spec(sink4.shape), _const_spec((hq, d)),
                  row_spec(d), _const_spec((1, d)), _const_spec((rows, d)), _const_spec((tq, tq))],
        out_specs=[row_spec(d),
                   pl.BlockSpec((8, tq), lambda b, i: (0, b * nt + i)),
                   pl.BlockSpec((tq, LANES), lambda b, i: (b * nt + i, 0)),
                   pl.BlockSpec((rows, LANES), lambda b, i: (0, 0)),
                   pl.BlockSpec((1, rows, LANES), lambda b, i: (b * nt + i, 0, 0))],
        out_shape=[jax.ShapeDtypeStruct((bsz, L, d), F32),
                   jax.ShapeDtypeStruct((8, bsz * L), I32),
                   jax.ShapeDtypeStruct((bsz * L, LANES), F32),
                   jax.ShapeDtypeStruct((rows, LANES), F32),
                   jax.ShapeDtypeStruct((bsz * nt, rows, LANES), F32)],
        scratch_shapes=[pltpu.VMEM((tq, hq), BF16), pltpu.VMEM((rows, LANES), F32)],
        compiler_params=_params(("arbitrary", "arbitrary")),
        name="swa_attn",
    )(q.reshape(bsz, L, hq), kv.reshape(bsz, L, -1), kv.reshape(bsz, L, -1), bias4, sink4,
      w_o.astype(BF16), h, ffn_norm_w.reshape(1, d), rwt, tri)


def _route(h, first, nw_ref, rw_ref, tri_ref, ei_ref, wcol_ref, cnt_ref, stat_ref, carry_ref, tr):
    @pl.when(first)
    def _():
        carry_ref[...] = jnp.zeros_like(carry_ref)

    hn = _rms(h, nw_ref[...])
    rows = 2 * N_EXPERTS
    lt = lax.dot_general(rw_ref[...], hn.astype(BF16), (((1,), (1,)), ((), ())),
                         preferred_element_type=F32)
    sub = lax.broadcasted_iota(I32, (rows, tr), 0)
    lt = jnp.where(sub < N_EXPERTS, lt, -jnp.inf)
    m1 = jnp.max(lt, axis=0, keepdims=True)
    i1 = jnp.min(jnp.where(lt == m1, sub, rows), axis=0, keepdims=True)
    lt2 = jnp.where(sub == i1, -jnp.inf, lt)
    m2 = jnp.max(lt2, axis=0, keepdims=True)
    i2 = jnp.min(jnp.where(lt2 == m2, sub, rows), axis=0, keepdims=True)
    e2 = jnp.exp(m2 - m1)
    w1 = 1.0 / (1.0 + e2)
    w2 = e2 / (1.0 + e2)
    hit = jnp.logical_or(sub == i1, sub == i2).astype(F32)
    pref = jnp.dot(hit.astype(BF16), tri_ref[...], preferred_element_type=F32)
    tile_cnt = jnp.sum(hit, axis=1, keepdims=True)
    seg_rows = jnp.floor((tile_cnt + (SEG_ALIGN - 1)) * (1.0 / SEG_ALIGN)) * SEG_ALIGN
    seg_rows = jnp.broadcast_to(seg_rows, (rows, LANES))
    below = jnp.zeros((rows, LANES), F32)
    sub_l = lax.broadcasted_iota(I32, (rows, LANES), 0)
    for e in range(N_EXPERTS - 1):
        below = below + jnp.where(sub_l > e, seg_rows[e:e + 1, :], 0.0)
    pos = pref + below[:, 0:1]
    r1 = jnp.sum(jnp.where(sub == i1, pos, 0.0), axis=0, keepdims=True)
    r2 = jnp.sum(jnp.where(sub == i2, pos, 0.0), axis=0, keepdims=True)
    before = carry_ref[...]
    lane = lax.broadcasted_iota(I32, (rows, LANES), 1)
    stat_ref[0] = jnp.where(lane == 0, seg_rows, before)
    total = before + seg_rows
    carry_ref[...] = total
    cnt_ref[...] = total
    ei = jnp.where(sub == 0, i1, jnp.where(sub == 1, i2, jnp.where(
        sub == 2, r1.astype(I32), jnp.where(sub == 3, r2.astype(I32), 0))))
    ei_ref[...] = ei[0:8]
    wrow = jnp.where(sub == 0, w1, jnp.where(sub == 1, w2, jnp.where(
        sub == 2, r1, jnp.where(sub == 3, r2, 0.0))))
    wpad = jnp.concatenate([wrow, jnp.zeros((LANES - rows, tr), F32)], axis=0)
    wcol_ref[...] = wpad.T


def _sorted_rows(tile):
    return 2 * tile + LANES


def _segment_pieces(step, slen_ref, soff_ref, sdst_ref, rows):
    top = rows.bit_length() - 1
    for e in range(N_EXPERTS):
        n = slen_ref[step * N_EXPERTS + e]
        off = soff_ref[step * N_EXPERTS + e]
        dst = sdst_ref[step * N_EXPERTS + e]
        for k in range(top, SEG_ALIGN.bit_length() - 2, -1):
            done = (n >> (k + 1)) << (k + 1)
            yield ((n & (1 << k)) != 0, pl.multiple_of(off + done, SEG_ALIGN),
                   pl.multiple_of(dst + done, SEG_ALIGN), 1 << k)


def _pad_pieces(pad_ref):
    for e in range(N_EXPERTS):
        start = pad_ref[2 * e]
        n = pad_ref[2 * e + 1]
        for k in range(MOE_ROWS.bit_length() - 2, SEG_ALIGN.bit_length() - 2, -1):
            done = (n >> (k + 1)) << (k + 1)
            yield (n & (1 << k)) != 0, pl.multiple_of(start + done, SEG_ALIGN), 1 << k


def _dispatch_kernel(slen_ref, soff_ref, sdst_ref, pad_ref, h_ref, nw_ref, ei_ref, xs_ref,
                     srt_ref, zero_ref, sem, pad_sem, *, td):
    step = pl.program_id(0)
    last = pl.num_programs(0) - 1
    slot = lax.rem(step, 2)

    def pad_copy(dst_row, rows):
        return pltpu.make_async_copy(zero_ref.at[pl.ds(0, rows)], xs_ref.at[pl.ds(dst_row, rows)], pad_sem)

    def tail_copy(j):
        return pad_copy(pl.multiple_of(pad_ref[2 * N_EXPERTS] + j * MOE_ROWS, SEG_ALIGN), MOE_ROWS)

    @pl.when(step == 0)
    def _():
        zero_ref[...] = jnp.zeros_like(zero_ref)
        for present, dst_row, rows in _pad_pieces(pad_ref):
            @pl.when(present)
            def _(dst_row=dst_row, rows=rows):
                pad_copy(dst_row, rows).start()

        def start_tail(j, carry):
            tail_copy(j).start()
            return carry

        lax.fori_loop(0, pad_ref[2 * N_EXPERTS + 1], start_tail, 0)

    hn = _rms(h_ref[...], nw_ref[...]).astype(BF16)
    row = lax.broadcasted_iota(I32, (_sorted_rows(td), td), 0)
    pick = jnp.logical_or(row == ei_ref[2:3, :], row == ei_ref[3:4, :])
    srt_ref[slot] = jnp.dot(jnp.where(pick, 1.0, 0.0).astype(BF16), hn, preferred_element_type=F32)

    def copy(ring_slot, src_row, dst_row, rows):
        return pltpu.make_async_copy(srt_ref.at[ring_slot, pl.ds(src_row, rows)],
                                     xs_ref.at[pl.ds(dst_row, rows)], sem.at[ring_slot])

    def for_pieces(of_step, ring_slot, act):
        for present, src_row, dst_row, rows in _segment_pieces(of_step, slen_ref, soff_ref, sdst_ref, td):
            @pl.when(present)
            def _(src_row=src_row, dst_row=dst_row, rows=rows):
                act(copy(ring_slot, src_row, dst_row, rows))

    for_pieces(step, slot, lambda c: c.start())

    @pl.when(step > 0)
    def _():
        for_pieces(step - 1, 1 - slot, lambda c: c.wait())

    @pl.when(step == last)
    def _():
        for_pieces(step, slot, lambda c: c.wait())

    @pl.when(step == 0)
    def _():
        for present, dst_row, rows in _pad_pieces(pad_ref):
            @pl.when(present)
            def _(dst_row=dst_row, rows=rows):
                pad_copy(dst_row, rows).wait()

        def wait_tail(j, carry):
            tail_copy(j).wait()
            return carry

        lax.fori_loop(0, pad_ref[2 * N_EXPERTS + 1], wait_tail, 0)


def _stage_dispatch(seg_len, seg_off, seg_dst, pads, h, norm_w, ei, n_slots):
    t, d = h.shape
    td = _tile(t, ROW_TILE)
    return pl.pallas_call(
        functools.partial(_dispatch_kernel, td=td),
        grid_spec=pltpu.PrefetchScalarGridSpec(
            num_scalar_prefetch=4, grid=(t // td,),
            in_specs=[pl.BlockSpec((td, d), lambda i, *_: (i, 0)),
                      pl.BlockSpec((1, d), lambda i, *_: (0, 0)),
                      pl.BlockSpec((8, td), lambda i, *_: (0, i))],
            out_specs=pl.BlockSpec(memory_space=pl.ANY),
            scratch_shapes=[pltpu.VMEM((2, _sorted_rows(td), d), F32), pltpu.VMEM((MOE_ROWS, d), F32),
                            pltpu.SemaphoreType.DMA((2,)), pltpu.SemaphoreType.DMA]),
        out_shape=jax.ShapeDtypeStruct((n_slots, d), F32),
        compiler_params=pltpu.CompilerParams(dimension_semantics=("arbitrary",), has_side_effects=True,
                                             vmem_limit_bytes=VMEM_LIMIT),
        name="moe_dispatch",
    )(seg_len, seg_off, seg_dst, pads, h, norm_w.reshape(1, d), ei)


def _moe_kernel(be_ref, nu_ref, xs_ref, wg_ref, wu_ref, wd_ref, ys_ref, xb_ref, hd_ref, *, tf):
    del be_ref
    f = pl.program_id(1)

    @pl.when(pl.program_id(0) < nu_ref[0])
    def _():
        @pl.when(f == 0)
        def _():
            xb_ref[...] = xs_ref[...].astype(BF16)

        x = xb_ref[...]
        for c in range(tf // MOE_SUB):
            cs = slice(c * MOE_SUB, (c + 1) * MOE_SUB)
            g = jnp.dot(x, wg_ref[0, :, cs], preferred_element_type=F32)
            u = jnp.dot(x, wu_ref[0, :, cs], preferred_element_type=F32)
            hd_ref[:, cs] = (_silu(g) * u).astype(BF16)
        y = jnp.dot(hd_ref[...], wd_ref[0], preferred_element_type=F32)

        @pl.when(f == 0)
        def _():
            ys_ref[...] = y

        @pl.when(f > 0)
        def _():
            ys_ref[...] += y

    @pl.when(jnp.logical_and(pl.program_id(0) >= nu_ref[0], f == 0))
    def _():
        ys_ref[...] = jnp.zeros_like(ys_ref)


def _stage_moe(blk_e, n_used, xs, w_gate, w_up, w_down):
    n_slots, d = xs.shape
    d_ff = w_gate.shape[-1]
    tm = MOE_ROWS
    tf = _tile(d_ff, MOE_COLS)
    nb, nf = n_slots // tm, d_ff // tf

    def blk(b, nu):
        return jnp.minimum(b, nu[0] - 1)

    def col(b, f, nu):
        return jnp.where(b < nu[0], f, nf - 1)

    return pl.pallas_call(
        functools.partial(_moe_kernel, tf=tf),
        grid_spec=pltpu.PrefetchScalarGridSpec(
            num_scalar_prefetch=2, grid=(nb, nf),
            in_specs=[pl.BlockSpec((tm, d), lambda b, f, be, nu: (blk(b, nu), 0)),
                      pl.BlockSpec((1, d, tf), lambda b, f, be, nu: (be[blk(b, nu)], 0, col(b, f, nu))),
                      pl.BlockSpec((1, d, tf), lambda b, f, be, nu: (be[blk(b, nu)], 0, col(b, f, nu))),
                      pl.BlockSpec((1, tf, d), lambda b, f, be, nu: (be[blk(b, nu)], col(b, f, nu), 0))],
            out_specs=pl.BlockSpec((tm, d), lambda b, f, be, nu: (b, 0)),
            scratch_shapes=[pltpu.VMEM((tm, d), BF16), pltpu.VMEM((tm, tf), BF16)]),
        out_shape=jax.ShapeDtypeStruct((n_slots, d), F32),
        compiler_params=_params(("arbitrary", "arbitrary")),
        name="moe_experts",
    )(blk_e, n_used, xs, w_gate.astype(BF16), w_up.astype(BF16), w_down.astype(BF16))


def _combine_kernel(slen_ref, soff_ref, sdst_ref, h_ref, w_ref, nw_ref, ys_ref, o_ref, buf_ref, sem,
                    *, tc):
    i = pl.program_id(0)
    n = pl.num_programs(0)

    def copy(step_slot, src_row, dst_row, rows):
        return pltpu.make_async_copy(ys_ref.at[pl.ds(src_row, rows)],
                                     buf_ref.at[step_slot, pl.ds(dst_row, rows)], sem.at[step_slot])

    def gather(step, step_slot):
        for present, off, dst, rows in _segment_pieces(step, slen_ref, soff_ref, sdst_ref, tc):
            @pl.when(present)
            def _(off=off, dst=dst, rows=rows):
                copy(step_slot, dst, off, rows).start()

    @pl.when(i == 0)
    def _():
        buf_ref[:, 2 * tc:, :] = jnp.zeros((2, _sorted_rows(tc) - 2 * tc, buf_ref.shape[-1]), F32)
        gather(0, 0)

    slot = lax.rem(i, 2)

    @pl.when(i + 1 < n)
    def _():
        gather(i + 1, 1 - slot)

    for present, off, dst, rows in _segment_pieces(i, slen_ref, soff_ref, sdst_ref, tc):
        @pl.when(present)
        def _(off=off, dst=dst, rows=rows):
            copy(slot, dst, off, rows).wait()

    yb = buf_ref[slot].astype(BF16)
    pos = lax.broadcasted_iota(I32, (tc, _sorted_rows(tc)), 1)
    sel0 = jnp.where(pos == w_ref[:, 2:3].astype(I32), 1.0, 0.0).astype(BF16)
    sel1 = jnp.where(pos == w_ref[:, 3:4].astype(I32), 1.0, 0.0).astype(BF16)
    y0 = jnp.dot(sel0, yb, preferred_element_type=F32)
    y1 = jnp.dot(sel1, yb, preferred_element_type=F32)
    h = h_ref[...] + (y0 * w_ref[:, 0:1] + y1 * w_ref[:, 1:2])
    o_ref[...] = _rms(h, nw_ref[...])


def _stage_combine(seg_len, seg_off, seg_dst, h, wcol, norm_w, ys):
    t, d = h.shape
    tc = _tile(t, ROW_TILE)
    return pl.pallas_call(
        functools.partial(_combine_kernel, tc=tc),
        grid_spec=pltpu.PrefetchScalarGridSpec(
            num_scalar_prefetch=3, grid=(t // tc,),
            in_specs=[pl.BlockSpec((tc, d), lambda i, *_: (i, 0)),
                      pl.BlockSpec((tc, LANES), lambda i, *_: (i, 0)),
                      pl.BlockSpec((1, d), lambda i, *_: (0, 0)),
                      pl.BlockSpec(memory_space=pl.ANY)],
            out_specs=pl.BlockSpec((tc, d), lambda i, *_: (i, 0)),
            scratch_shapes=[pltpu.VMEM((2, _sorted_rows(tc), d), F32), pltpu.SemaphoreType.DMA((2,))]),
        out_shape=jax.ShapeDtypeStruct((t, d), F32),
        compiler_params=_params(("arbitrary",)),
        name="moe_combine",
    )(seg_len, seg_off, seg_dst, h, wcol, norm_w.reshape(1, d), ys)


def kernel(x, attn_norm_w, ffn_norm_w, a_w_in, a_conv_w, a_a_log, a_dt_bias, a_onorm_w, a_w_out,
           kv_norm_w, w_kv, b_w_q, b_sinks, b_w_o, rel_bias, ffn_w_gate, ffn_w_up, ffn_w_down,
           moe_router, moe_w_gate, moe_w_up, moe_w_down, final_norm_w):
    bsz, L, d = x.shape
    t = bsz * L

    q, k, v, gate, bg, gct = _stage_in(x, attn_norm_w[0], a_w_in[0], a_conv_w[0], a_a_log[0],
                                       a_dt_bias[0])
    og = _stage_delta(q, k, v, gate, bg, gct, a_onorm_w[0])
    h, q2, kv2 = _stage_ffn(og.reshape(t, A_WIDTH), x.reshape(t, d), a_w_out[0], ffn_norm_w[0],
                            ffn_w_gate[0], ffn_w_up[0], ffn_w_down[0], attn_norm_w[1], kv_norm_w,
                            b_w_q[0], w_kv)

    bias4 = _stage_bias(rel_bias)
    h, ei, wcol, cnt, stat = _stage_attn(q2, kv2, bias4, b_sinks[0], b_w_o[0], h.reshape(bsz, L, d),
                                         ffn_norm_w[1], moe_router[0])
    h = h.reshape(t, d)

    counts = cnt[:N_EXPERTS, 0].astype(I32)
    p_counts = (counts + MOE_ROWS - 1) // MOE_ROWS * MOE_ROWS
    p_end = jnp.cumsum(p_counts)
    p_start = p_end - p_counts
    tile_cnt = stat[:, :N_EXPERTS, 0].astype(I32)
    before = stat[:, :N_EXPERTS, 1].astype(I32)
    seg_len = tile_cnt.reshape(-1)
    seg_off = (jnp.cumsum(tile_cnt, axis=1) - tile_cnt).reshape(-1)
    seg_dst = (p_start[None, :] + before).reshape(-1)
    n_tiles = stat.shape[0]
    n_blocks = -(-(2 * t + n_tiles * N_EXPERTS * (SEG_ALIGN - 1)) // MOE_ROWS) + N_EXPERTS
    blk_e = jnp.minimum(jnp.searchsorted(p_end, jnp.arange(n_blocks, dtype=I32) * MOE_ROWS, side='right'),
                        N_EXPERTS - 1).astype(I32)
    n_used = (p_end[-1:] // MOE_ROWS).astype(I32)
    pad_start = p_start + counts
    pads = jnp.concatenate([jnp.stack([pad_start, p_end - pad_start], axis=1).reshape(-1),
                            p_end[-1:], n_blocks - n_used]).astype(I32)
    xs = _stage_dispatch(seg_len, seg_off, seg_dst, pads, h, ffn_norm_w[1], ei, n_blocks * MOE_ROWS)
    ys = _stage_moe(blk_e, n_used, xs, moe_w_gate[0], moe_w_up[0], moe_w_down[0])
    out = _stage_combine(seg_len, seg_off, seg_dst, h, wcol, final_norm_w, ys)
    return out.reshape(bsz, L, d)
```

```python
import functools
import math

import numpy as np
import jax
import jax.numpy as jnp
from jax import lax
from jax.experimental import pallas as pl
from jax.experimental.pallas import tpu as pltpu

F32 = jnp.float32
BF16 = jnp.bfloat16
I32 = jnp.int32

EPS = 1e-6
CHUNK = 64
A_HEADS = 8
A_HEAD_DIM = 128
A_WIDTH = A_HEADS * A_HEAD_DIM
A_CONV = 4
B_Q_HEADS = 16
B_KV_HEADS = 4
B_HEAD_DIM = 64
B_GROUP = B_Q_HEADS // B_KV_HEADS
WINDOW_CHUNKS = 2
BAND = (WINDOW_CHUNKS + 1) * CHUNK
REL_BUCKETS = 32
REL_MAX_DIST = 128
N_EXPERTS = 8
LANES = 128

VMEM_LIMIT = 56 * 1024 * 1024

DELTA_TILE = 1024
ROW_TILE = 512
CONV_COLS = 512
FFN_COLS = 256
MOE_ROWS = 1024
MOE_COLS = 1792
SEG_ALIGN = 8
MOE_SUB = 256
ATT_Q = 2 * CHUNK
LOCAL_CHUNKS = 2


def _tile(n, pref):
    t = min(n, pref)
    assert n % t == 0, (n, t)
    return t


def _mm(a, b):
    return jnp.dot(a.astype(BF16), b.astype(BF16), preferred_element_type=F32)


def _mm_nt(a, b):
    return lax.dot_general(a.astype(BF16), b.astype(BF16), (((1,), (1,)), ((), ())),
                           preferred_element_type=F32)


def _mm_tn(a, b):
    return lax.dot_general(a.astype(BF16), b.astype(BF16), (((0,), (0,)), ((), ())),
                           preferred_element_type=F32)


def _silu(x):
    return x * jax.nn.sigmoid(x)


def _rms(x, w):
    return x * lax.rsqrt(jnp.mean(x * x, axis=-1, keepdims=True) + EPS) * w


def _const_spec(shape):
    nd = len(shape)
    return pl.BlockSpec(shape, lambda *_: (0,) * nd, pipeline_mode=pl.Buffered(1))


def _params(sem):
    return pltpu.CompilerParams(dimension_semantics=sem, vmem_limit_bytes=VMEM_LIMIT)


def _in_kernel(x_ref, nw_ref, w_ref, wba_ref, cw_ref, ap_ref, seg_ref,
               q_ref, k_ref, v_ref, gate_ref, bg_ref, gct_ref, ext_ref, *, tm):
    @pl.when(pl.program_id(1) == 0)
    def _():
        ext_ref[0:8, :] = jnp.zeros((8, 3 * A_WIDTH), F32)

    hn = _rms(x_ref[0], nw_ref[...]).astype(BF16)

    ba = jnp.dot(hn, wba_ref[...], preferred_element_type=F32)
    beta = jax.nn.sigmoid(ba)
    z = ba + ap_ref[1:2, :]
    softplus = jnp.maximum(z, 0.0) + jnp.log1p(jnp.exp(-jnp.abs(z)))
    g = -jnp.exp(ap_ref[0:1, :]) * softplus
    g1 = g.astype(BF16)
    r1 = g - g1.astype(F32)
    g2 = r1.astype(BF16)
    g3 = (r1 - g2.astype(F32)).astype(BF16)
    seg = seg_ref[...]
    gc = (jnp.dot(seg, g1, preferred_element_type=F32) + jnp.dot(seg, g2, preferred_element_type=F32)
          + jnp.dot(seg, g3, preferred_element_type=F32))
    lane = lax.broadcasted_iota(I32, (tm, LANES), 1)
    bg = jnp.where(lane < A_HEADS, beta, gc)
    bg_ref[0] = bg
    bgt = bg.T
    for ci in range(tm // CHUNK):
        gct_ref[0, ci] = bgt[0:2 * A_HEADS, ci * CHUNK:(ci + 1) * CHUNK]

    outs = (q_ref, k_ref, v_ref)
    for c in range(3 * A_WIDTH // CONV_COLS):
        cs = slice(c * CONV_COLS, (c + 1) * CONV_COLS)
        p = jnp.dot(hn, w_ref[:, cs], preferred_element_type=F32)
        ext_ref[8:tm + 8, cs] = p
        acc = cw_ref[A_CONV - 1:A_CONV, cs] * p
        for j in range(A_CONV - 1):
            acc = acc + cw_ref[j:j + 1, cs] * ext_ref[5 + j:5 + j + tm, cs]
        ext_ref[0:8, cs] = p[tm - 8:, :]
        a = _silu(acc)
        which, off = divmod(c * CONV_COLS, A_WIDTH)
        for hh in range(CONV_COLS // A_HEAD_DIM):
            hs = a[:, hh * A_HEAD_DIM:(hh + 1) * A_HEAD_DIM]
            if which < 2:
                scale = lax.rsqrt(jnp.sum(hs * hs, axis=-1, keepdims=True) + EPS)
                hs = hs * (scale * (A_HEAD_DIM ** -0.5) if which == 0 else scale)
            lo = off + hh * A_HEAD_DIM
            outs[which][0, :, lo:lo + A_HEAD_DIM] = hs.astype(BF16)

    for c in range(A_WIDTH // CONV_COLS):
        cs = slice(3 * A_WIDTH + c * CONV_COLS, 3 * A_WIDTH + (c + 1) * CONV_COLS)
        gate_ref[0, :, c * CONV_COLS:(c + 1) * CONV_COLS] = jnp.dot(
            hn, w_ref[:, cs], preferred_element_type=F32)


def _stage_in(x, norm_w, w_in, conv_w, a_log, dt_bias):
    bsz, L, d = x.shape
    tm = _tile(L, ROW_TILE)
    nc = L // CHUNK
    w_main = w_in[:, :4 * A_WIDTH].astype(BF16)
    w_ba = jnp.pad(w_in[:, 4 * A_WIDTH:], ((0, 0), (0, LANES - 2 * A_HEADS))).astype(BF16)
    ap = jnp.zeros((2, LANES), F32)
    ap = ap.at[0, A_HEADS:2 * A_HEADS].set(a_log.astype(F32))
    ap = ap.at[1, A_HEADS:2 * A_HEADS].set(dt_bias.astype(F32))
    r = np.arange(tm)
    seg = jnp.asarray(((r[:, None] >= r[None, :]) &
                       (r[:, None] // CHUNK == r[None, :] // CHUNK)).astype(np.float32)).astype(BF16)
    row_spec = lambda w: pl.BlockSpec((1, tm, w), lambda b, i: (b, i, 0))
    act = jax.ShapeDtypeStruct((bsz, L, A_WIDTH), BF16)
    return pl.pallas_call(
        functools.partial(_in_kernel, tm=tm),
        grid=(bsz, L // tm),
        in_specs=[row_spec(d), _const_spec((1, d)), _const_spec((d, 4 * A_WIDTH)),
                  _const_spec((d, LANES)), _const_spec((A_CONV, 3 * A_WIDTH)),
                  _const_spec((2, LANES)), _const_spec((tm, tm))],
        out_specs=[row_spec(A_WIDTH), row_spec(A_WIDTH), row_spec(A_WIDTH), row_spec(A_WIDTH),
                   row_spec(LANES),
                   pl.BlockSpec((1, tm // CHUNK, 2 * A_HEADS, CHUNK), lambda b, i: (b, i, 0, 0))],
        out_shape=[act, act, act, jax.ShapeDtypeStruct((bsz, L, A_WIDTH), F32),
                   jax.ShapeDtypeStruct((bsz, L, LANES), F32),
                   jax.ShapeDtypeStruct((bsz, nc, 2 * A_HEADS, CHUNK), F32)],
        scratch_shapes=[pltpu.VMEM((tm + 8, 3 * A_WIDTH), F32)],
        compiler_params=_params(("arbitrary", "arbitrary")),
        name="in_proj",
    )(x, norm_w.reshape(1, d), w_main, w_ba, conv_w.astype(F32), ap, seg)


def _chunk_rows(c):
    if isinstance(c, int):
        return slice(c * CHUNK, (c + 1) * CHUNK)
    return pl.ds(pl.multiple_of(c * CHUNK, CHUNK), CHUNK)


def _delta_kernel(q_ref, k_ref, v_ref, gate_ref, bg_ref, gct_ref, ow_ref, o_ref,
                  s_ref, u_ref, w_ref, qd_ref, kd_ref, at_ref, *, cb):
    @pl.when(pl.program_id(1) == 0)
    def _():
        s_ref[...] = jnp.zeros_like(s_ref)

    heads = range(A_HEADS)
    hcols = [slice(h * A_HEAD_DIM, (h + 1) * A_HEAD_DIM) for h in heads]
    row = lax.broadcasted_iota(I32, (CHUNK, CHUNK), 0)
    col = lax.broadcasted_iota(I32, (CHUNK, CHUNK), 1)
    tri = row >= col
    strict = row > col
    diag = (row >> 4) == (col >> 4)
    eye = (row == col).astype(F32)

    def local(cc):
        items = [(j, h) for j in range(LOCAL_CHUNKS) for h in heads]
        its = range(len(items))
        cidx = [cc * LOCAL_CHUNKS + j for j in range(LOCAL_CHUNKS)]
        crow = [_chunk_rows(c) for c in cidx]
        rows = [crow[j] for j, _ in items]
        cols = [hcols[h] for _, h in items]
        qb = [q_ref[0, rows[i], cols[i]] for i in its]
        kb = [k_ref[0, rows[i], cols[i]] for i in its]
        vb = [v_ref[0, rows[i], cols[i]] for i in its]
        bgs = [bg_ref[0, crow[j], :] for j in range(LOCAL_CHUNKS)]
        gts = [gct_ref[0, c] for c in cidx]
        beta = [bgs[j][:, h:h + 1] for j, h in items]
        gcol = [bgs[j][:, A_HEADS + h:A_HEADS + h + 1] for j, h in items]
        grow = [gts[j][A_HEADS + h:A_HEADS + h + 1, :] for j, h in items]

        decay = [jnp.exp(jnp.where(tri, gcol[i] - grow[i], -jnp.inf)) for i in its]
        eg = [jnp.exp(gcol[i]) for i in its]
        kf = [kb[i].astype(F32) for i in its]
        kbeta = [kf[i] * beta[i] for i in its]
        lower = [jnp.where(strict, _mm_nt(kbeta[i], kb[i]) * decay[i], 0.0) for i in its]
        ld = [jnp.where(diag, lower[i], 0.0) for i in its]
        lo = [lower[i] - ld[i] for i in its]
        p = [eye - ld[i] for i in its]
        s = [_mm(ld[i], ld[i]) for i in its]
        for _ in range(2):
            p = [p[i] + _mm(p[i], s[i]) for i in its]
            s = [_mm(s[i], s[i]) for i in its]
        td = [p[i] + _mm(p[i], s[i]) for i in its]
        rhs = [jnp.concatenate([vb[i].astype(F32) * beta[i], kbeta[i] * eg[i]], axis=1) for i in its]
        r = [_mm(td[i], rhs[i]) for i in its]
        m = [_mm(td[i], lo[i]) for i in its]
        m2 = [_mm(m[i], m[i]) for i in its]
        r = [r[i] + _mm(m2[i], r[i]) for i in its]
        uw = [r[i] - _mm(m[i], r[i]) for i in its]
        attn = [_mm_nt(qb[i], kb[i]) * decay[i] for i in its]
        qd = [qb[i].astype(F32) * eg[i] for i in its]
        kd = [kf[i] * jnp.exp(grow[i][:, CHUNK - 1:CHUNK] - gcol[i]) for i in its]

        def store():
            for i, (_, h) in enumerate(items):
                u_ref[rows[i], cols[i]] = uw[i][:, :A_HEAD_DIM]
                w_ref[rows[i], cols[i]] = uw[i][:, A_HEAD_DIM:].astype(BF16)
                qd_ref[rows[i], cols[i]] = qd[i].astype(BF16)
                kd_ref[rows[i], cols[i]] = kd[i].astype(BF16)
                at_ref[h, rows[i], :] = attn[i].astype(BF16)

        return store

    def recur(cc):
        cidx = [cc * LOCAL_CHUNKS + j for j in range(LOCAL_CHUNKS)]
        crow = [_chunk_rows(c) for c in cidx]
        wq = [[jnp.concatenate([w_ref[r, hcols[h]], qd_ref[r, hcols[h]]], axis=0) for h in heads]
              for r in crow]
        u = [[u_ref[r, hcols[h]] for h in heads] for r in crow]
        kd = [[kd_ref[r, hcols[h]] for h in heads] for r in crow]
        at = [[at_ref[h, r, :] for h in heads] for r in crow]
        gate = [[gate_ref[0, r, hcols[h]] for h in heads] for r in crow]
        gts = [gct_ref[0, c] for c in cidx]
        st = [s_ref[h] for h in heads]
        ow = ow_ref[...]

        og = []
        for j in range(LOCAL_CHUNKS):
            eglast = [jnp.exp(gts[j][A_HEADS + h:A_HEADS + h + 1, CHUNK - 1:CHUNK]) for h in heads]
            ws_qs = [_mm(wq[j][h], st[h]) for h in heads]
            vnew = [u[j][h] - ws_qs[h][:CHUNK] for h in heads]
            o = [ws_qs[h][CHUNK:] + _mm(at[j][h], vnew[h]) for h in heads]
            st = [st[h] * eglast[h] + _mm_tn(kd[j][h], vnew[h]) for h in heads]
            og.append([_rms(o[h], ow) * _silu(gate[j][h]) for h in heads])

        def store():
            for h in heads:
                s_ref[h] = st[h]
                for j in range(LOCAL_CHUNKS):
                    o_ref[0, crow[j], hcols[h]] = og[j][h].astype(BF16)

        return store

    def fused(cc, carry):
        store_local = local(cc + 1)
        store_recur = recur(cc)
        store_local()
        store_recur()
        return carry

    groups = cb // LOCAL_CHUNKS
    local(0)()
    lax.fori_loop(0, groups - 1, fused, 0)
    recur(groups - 1)()


def _stage_delta(q, k, v, gate, bg, gct, onorm_w):
    bsz, L, _ = q.shape
    tm = _tile(L, DELTA_TILE)
    cb = tm // CHUNK
    row_spec = lambda w: pl.BlockSpec((1, tm, w), lambda b, i: (b, i, 0))
    return pl.pallas_call(
        functools.partial(_delta_kernel, cb=cb),
        grid=(bsz, L // tm),
        in_specs=[row_spec(A_WIDTH), row_spec(A_WIDTH), row_spec(A_WIDTH), row_spec(A_WIDTH),
                  row_spec(LANES),
                  pl.BlockSpec((1, cb, 2 * A_HEADS, CHUNK), lambda b, i: (b, i, 0, 0)),
                  _const_spec((1, A_HEAD_DIM))],
        out_specs=row_spec(A_WIDTH),
        out_shape=jax.ShapeDtypeStruct((bsz, L, A_WIDTH), BF16),
        scratch_shapes=[pltpu.VMEM((A_HEADS, A_HEAD_DIM, A_HEAD_DIM), F32),
                        pltpu.VMEM((tm, A_WIDTH), F32), pltpu.VMEM((tm, A_WIDTH), BF16),
                        pltpu.VMEM((tm, A_WIDTH), BF16), pltpu.VMEM((tm, A_WIDTH), BF16),
                        pltpu.VMEM((A_HEADS, tm, CHUNK), BF16)],
        compiler_params=_params(("arbitrary", "arbitrary")),
        name="delta_rule",
    )(q, k, v, gate, bg, gct, onorm_w.reshape(1, A_HEAD_DIM).astype(F32))


def _ffn_kernel(og_ref, x_ref, wo_ref, fw_ref, wg_ref, wu_ref, wd_ref, aw_ref, kw_ref, wq_ref,
                wkv_ref, h_ref, q_ref, kv_ref, hd_ref, *, d_ff):
    h1 = x_ref[...] + jnp.dot(og_ref[...], wo_ref[...], preferred_element_type=F32)
    hn = _rms(h1, fw_ref[...]).astype(BF16)
    for c in range(d_ff // FFN_COLS):
        cs = slice(c * FFN_COLS, (c + 1) * FFN_COLS)
        g = jnp.dot(hn, wg_ref[:, cs], preferred_element_type=F32)
        u = jnp.dot(hn, wu_ref[:, cs], preferred_element_type=F32)
        hd_ref[:, cs] = (_silu(g) * u).astype(BF16)
    h2 = h1 + jnp.dot(hd_ref[...], wd_ref[...], preferred_element_type=F32)
    h_ref[...] = h2
    xh = h2 * lax.rsqrt(jnp.mean(h2 * h2, axis=-1, keepdims=True) + EPS)
    q = jnp.dot((xh * aw_ref[...]).astype(BF16), wq_ref[...], preferred_element_type=F32)
    q_ref[...] = (q * (B_HEAD_DIM ** -0.5)).astype(BF16)
    kv_ref[...] = jnp.dot((xh * kw_ref[...]).astype(BF16), wkv_ref[...],
                          preferred_element_type=F32).astype(BF16)


def _stage_ffn(og, x, w_out, ffn_norm_w, w_gate, w_up, w_down, attn_norm_w, kv_norm_w, w_q, w_kv):
    t, d = x.shape
    d_ff = w_gate.shape[1]
    tm = _tile(t, ROW_TILE)
    hq = B_Q_HEADS * B_HEAD_DIM
    half = B_KV_HEADS * B_HEAD_DIM
    dup = lambda w: jnp.concatenate([w.reshape(d, B_KV_HEADS, 1, B_HEAD_DIM)] * 2, axis=2).reshape(d, 2 * half)
    w_kvd = jnp.concatenate([dup(w_kv[:, :half]), dup(w_kv[:, half:])], axis=1).astype(BF16)
    row_spec = lambda w: pl.BlockSpec((tm, w), lambda i: (i, 0))
    return pl.pallas_call(
        functools.partial(_ffn_kernel, d_ff=d_ff),
        grid=(t // tm,),
        in_specs=[row_spec(A_WIDTH), row_spec(d), _const_spec((A_WIDTH, d)), _const_spec((1, d)),
                  _const_spec((d, d_ff)), _const_spec((d, d_ff)), _const_spec((d_ff, d)),
                  _const_spec((1, d)), _const_spec((1, d)), _const_spec((d, hq)),
                  _const_spec((d, 4 * half))],
        out_specs=[row_spec(d), row_spec(hq), row_spec(4 * half)],
        out_shape=[jax.ShapeDtypeStruct((t, d), F32), jax.ShapeDtypeStruct((t, hq), BF16),
                   jax.ShapeDtypeStruct((t, 4 * half), BF16)],
        scratch_shapes=[pltpu.VMEM((tm, d_ff), BF16)],
        compiler_params=_params(("arbitrary",)),
        name="ffn_dense",
    )(og, x, w_out.astype(BF16), ffn_norm_w.reshape(1, d), w_gate.astype(BF16), w_up.astype(BF16),
      w_down.astype(BF16), attn_norm_w.reshape(1, d), kv_norm_w.reshape(1, d), w_q.astype(BF16), w_kvd)


def _rel_buckets(rel):
    nb = REL_BUCKETS // 2
    max_exact = nb // 2
    ret = jnp.where(rel > 0, nb, 0)
    dist = jnp.abs(rel)
    dist_f = jnp.maximum(dist, 1).astype(F32)
    large = max_exact + (jnp.log(dist_f / max_exact) / math.log(REL_MAX_DIST / max_exact)
                         * (nb - max_exact)).astype(I32)
    large = jnp.minimum(large, nb - 1)
    return ret + jnp.where(dist < max_exact, dist, large)


def _bias_kernel(rb_ref, bk_ref, o_ref):
    bk = bk_ref[...]
    for h in range(B_Q_HEADS):
        acc = jnp.zeros(bk.shape, F32)
        for b in range(REL_BUCKETS):
            acc = jnp.where(bk == b, rb_ref[b * B_Q_HEADS + h], acc)
        o_ref[h] = acc


def _stage_bias(rel_bias):
    rel = (jnp.arange(BAND)[None, :] - WINDOW_CHUNKS * CHUNK) - jnp.arange(CHUNK)[:, None]
    buckets = _rel_buckets(rel).astype(I32)
    bias = pl.pallas_call(
        _bias_kernel,
        in_specs=[pl.BlockSpec(memory_space=pltpu.SMEM), pl.BlockSpec(memory_space=pltpu.VMEM)],
        out_specs=pl.BlockSpec(memory_space=pltpu.VMEM),
        out_shape=jax.ShapeDtypeStruct((B_Q_HEADS, CHUNK, BAND), F32),
        name="rel_bias",
    )(rel_bias.astype(F32).reshape(-1), buckets)
    pad = lambda lo, hi: jnp.pad(bias, ((0, 0), (0, 0), (lo, hi)), constant_values=-jnp.inf)
    pair = jnp.concatenate([pad(0, CHUNK), pad(CHUNK, 0)], axis=1)
    return pair.reshape(B_KV_HEADS, B_GROUP * ATT_Q, 2 * ATT_Q)


def _attn_kernel(q_ref, kv_ref, kvp_ref, bias_ref, sink_ref, wo_ref, h_ref, nw_ref, rw_ref, tri_ref,
                 o_ref, ei_ref, wcol_ref, cnt_ref, stat_ref, att_ref, carry_ref, *, nblk, tq):
    first = pl.program_id(1) == 0
    lane = lax.broadcasted_iota(I32, (ATT_Q, LANES), 1)
    lo_half = lane < B_HEAD_DIM
    kidx = lax.broadcasted_iota(I32, (B_GROUP * ATT_Q, 2 * ATT_Q), 1)
    koff = B_KV_HEADS * LANES
    for jb in range(nblk):
        rows = slice(jb * ATT_Q, (jb + 1) * ATT_Q)
        if jb == 0:
            keys = jnp.concatenate([kvp_ref[0], kv_ref[0, :ATT_Q, :]], axis=0)
        else:
            keys = kv_ref[0, (jb - 1) * ATT_Q:(jb + 1) * ATT_Q, :]
        for hk in range(B_KV_HEADS):
            kd = keys[:, hk * LANES:(hk + 1) * LANES]
            vd = keys[:, koff + hk * LANES:koff + (hk + 1) * LANES]
            qa = q_ref[0, rows, (2 * hk) * LANES:(2 * hk + 1) * LANES]
            qb = q_ref[0, rows, (2 * hk + 1) * LANES:(2 * hk + 2) * LANES]
            zero = jnp.zeros_like(qa)
            q4 = jnp.concatenate([jnp.where(lo_half, qa, zero), jnp.where(lo_half, zero, qa),
                                  jnp.where(lo_half, qb, zero), jnp.where(lo_half, zero, qb)], axis=0)
            s = lax.dot_general(q4, kd, (((1,), (1,)), ((), ())), preferred_element_type=F32)
            s = s + bias_ref[hk]
            if jb == 0:
                s = jnp.where(jnp.logical_and(first, kidx < ATT_Q), -jnp.inf, s)
            sink = sink_ref[hk]
            m = jnp.maximum(jnp.max(s, axis=-1, keepdims=True), sink)
            p = jnp.exp(s - m)
            denom = jnp.sum(p, axis=-1, keepdims=True) + jnp.exp(sink - m)
            o = jnp.dot(p.astype(BF16), vd, preferred_element_type=F32) / denom
            ta = jnp.where(lo_half, o[0:ATT_Q], o[ATT_Q:2 * ATT_Q])
            tb = jnp.where(lo_half, o[2 * ATT_Q:3 * ATT_Q], o[3 * ATT_Q:4 * ATT_Q])
            att_ref[rows, (2 * hk) * LANES:(2 * hk + 1) * LANES] = ta.astype(BF16)
            att_ref[rows, (2 * hk + 1) * LANES:(2 * hk + 2) * LANES] = tb.astype(BF16)
    h_out = h_ref[0] + jnp.dot(att_ref[...], wo_ref[...], preferred_element_type=F32)
    o_ref[0] = h_out
    _route(h_out, jnp.logical_and(pl.program_id(0) == 0, first), nw_ref, rw_ref, tri_ref,
           ei_ref, wcol_ref, cnt_ref, stat_ref, carry_ref, tq)


def _stage_attn(q, kv, bias4, sinks, w_o, h, ffn_norm_w, router_w):
    bsz, L, d = h.shape
    tq = _tile(L, ROW_TILE)
    nt = L // tq
    halo = WINDOW_CHUNKS * CHUNK
    per = tq // halo
    hq = B_Q_HEADS * B_HEAD_DIM
    sink4 = jnp.repeat(sinks.astype(F32).reshape(B_KV_HEADS, B_GROUP), ATT_Q, axis=1)[..., None]
    rows = 2 * N_EXPERTS
    rwt = jnp.pad(router_w.T, ((0, rows - N_EXPERTS), (0, 0))).astype(BF16)
    r = np.arange(tq)
    tri = jnp.asarray((r[:, None] < r[None, :]).astype(np.float32)).astype(BF16)
    row_spec = lambda w: pl.BlockSpec((1, tq, w), lambda b, i: (b, i, 0))
    return pl.pallas_call(
        functools.partial(_attn_kernel, nblk=tq // ATT_Q, tq=tq),
        grid=(bsz, nt),
        in_specs=[row_spec(hq), row_spec(kv.shape[-1]),
                  pl.BlockSpec((1, halo, kv.shape[-1]), lambda b, i: (b, jnp.maximum(i * per - 1, 0), 0)),
                  _const_spec(bias4.shape), _const_spec(sink4.shape), _const_spec((hq, d)),
                  row_spec(d), _const_spec((1, d)), _const_spec((rows, d)), _const_spec((tq, tq))],
        out_specs=[row_spec(d),
                   pl.BlockSpec((8, tq), lambda b, i: (0, b * nt + i)),
                   pl.BlockSpec((tq, LANES), lambda b, i: (b * nt + i, 0)),
                   pl.BlockSpec((rows, LANES), lambda b, i: (0, 0)),
                   pl.BlockSpec((1, rows, LANES), lambda b, i: (b * nt + i, 0, 0))],
        out_shape=[jax.ShapeDtypeStruct((bsz, L, d), F32),
                   jax.ShapeDtypeStruct((8, bsz * L), I32),
                   jax.ShapeDtypeStruct((bsz * L, LANES), F32),
                   jax.ShapeDtypeStruct((rows, LANES), F32),
                   jax.ShapeDtypeStruct((bsz * nt, rows, LANES), F32)],
        scratch_shapes=[pltpu.VMEM((tq, hq), BF16), pltpu.VMEM((rows, LANES), F32)],
        compiler_params=_params(("arbitrary", "arbitrary")),
        name="swa_attn",
    )(q.reshape(bsz, L, hq), kv.reshape(bsz, L, -1), kv.reshape(bsz, L, -1), bias4, sink4,
      w_o.astype(BF16), h, ffn_norm_w.reshape(1, d), rwt, tri)


def _route(h, first, nw_ref, rw_ref, tri_ref, ei_ref, wcol_ref, cnt_ref, stat_ref, carry_ref, tr):
    @pl.when(first)
    def _():
        carry_ref[...] = jnp.zeros_like(carry_ref)

    hn = _rms(h, nw_ref[...])
    rows = 2 * N_EXPERTS
    lt = lax.dot_general(rw_ref[...], hn.astype(BF16), (((1,), (1,)), ((), ())),
                         preferred_element_type=F32)
    sub = lax.broadcasted_iota(I32, (rows, tr), 0)
    lt = jnp.where(sub < N_EXPERTS, lt, -jnp.inf)
    m1 = jnp.max(lt, axis=0, keepdims=True)
    i1 = jnp.min(jnp.where(lt == m1, sub, rows), axis=0, keepdims=True)
    lt2 = jnp.where(sub == i1, -jnp.inf, lt)
    m2 = jnp.max(lt2, axis=0, keepdims=True)
    i2 = jnp.min(jnp.where(lt2 == m2, sub, rows), axis=0, keepdims=True)
    e2 = jnp.exp(m2 - m1)
    w1 = 1.0 / (1.0 + e2)
    w2 = e2 / (1.0 + e2)
    hit = jnp.logical_or(sub == i1, sub == i2).astype(F32)
    pref = jnp.dot(hit.astype(BF16), tri_ref[...], preferred_element_type=F32)
    tile_cnt = jnp.sum(hit, axis=1, keepdims=True)
    seg_rows = jnp.floor((tile_cnt + (SEG_ALIGN - 1)) * (1.0 / SEG_ALIGN)) * SEG_ALIGN
    seg_rows = jnp.broadcast_to(seg_rows, (rows, LANES))
    below = jnp.zeros((rows, LANES), F32)
    sub_l = lax.broadcasted_iota(I32, (rows, LANES), 0)
    for e in range(N_EXPERTS - 1):
        below = below + jnp.where(sub_l > e, seg_rows[e:e + 1, :], 0.0)
    pos = pref + below[:, 0:1]
    r1 = jnp.sum(jnp.where(sub == i1, pos, 0.0), axis=0, keepdims=True)
    r2 = jnp.sum(jnp.where(sub == i2, pos, 0.0), axis=0, keepdims=True)
    before = carry_ref[...]
    lane = lax.broadcasted_iota(I32, (rows, LANES), 1)
    stat_ref[0] = jnp.where(lane == 0, seg_rows, before)
    total = before + seg_rows
    carry_ref[...] = total
    cnt_ref[...] = total
    ei = jnp.where(sub == 0, i1, jnp.where(sub == 1, i2, jnp.where(
        sub == 2, r1.astype(I32), jnp.where(sub == 3, r2.astype(I32), 0))))
    ei_ref[...] = ei[0:8]
    wrow = jnp.where(sub == 0, w1, jnp.where(sub == 1, w2, jnp.where(
        sub == 2, r1, jnp.where(sub == 3, r2, 0.0))))
    wpad = jnp.concatenate([wrow, jnp.zeros((LANES - rows, tr), F32)], axis=0)
    wcol_ref[...] = wpad.T


def _sorted_rows(tile):
    return 2 * tile + LANES


def _segment_pieces(step, slen_ref, soff_ref, sdst_ref, rows):
    top = rows.bit_length() - 1
    for e in range(N_EXPERTS):
        n = slen_ref[step * N_EXPERTS + e]
        off = soff_ref[step * N_EXPERTS + e]
        dst = sdst_ref[step * N_EXPERTS + e]
        for k in range(top, SEG_ALIGN.bit_length() - 2, -1):
            done = (n >> (k + 1)) << (k + 1)
            yield ((n & (1 << k)) != 0, pl.multiple_of(off + done, SEG_ALIGN),
                   pl.multiple_of(dst + done, SEG_ALIGN), 1 << k)


def _pad_pieces(pad_ref):
    for e in range(N_EXPERTS):
        start = pad_ref[2 * e]
        n = pad_ref[2 * e + 1]
        for k in range(MOE_ROWS.bit_length() - 2, SEG_ALIGN.bit_length() - 2, -1):
            done = (n >> (k + 1)) << (k + 1)
            yield (n & (1 << k)) != 0, pl.multiple_of(start + done, SEG_ALIGN), 1 << k


def _dispatch_kernel(slen_ref, soff_ref, sdst_ref, pad_ref, h_ref, nw_ref, ei_ref, xs_ref,
                     srt_ref, zero_ref, sem, pad_sem, *, td):
    step = pl.program_id(0)
    last = pl.num_programs(0) - 1
    slot = lax.rem(step, 2)

    def pad_copy(dst_row, rows):
        return pltpu.make_async_copy(zero_ref.at[pl.ds(0, rows)], xs_ref.at[pl.ds(dst_row, rows)], pad_sem)

    def tail_copy(j):
        return pad_copy(pl.multiple_of(pad_ref[2 * N_EXPERTS] + j * MOE_ROWS, SEG_ALIGN), MOE_ROWS)

    @pl.when(step == 0)
    def _():
        zero_ref[...] = jnp.zeros_like(zero_ref)
        for present, dst_row, rows in _pad_pieces(pad_ref):
            @pl.when(present)
            def _(dst_row=dst_row, rows=rows):
                pad_copy(dst_row, rows).start()

        def start_tail(j, carry):
            tail_copy(j).start()
            return carry

        lax.fori_loop(0, pad_ref[2 * N_EXPERTS + 1], start_tail, 0)

    hn = _rms(h_ref[...], nw_ref[...]).astype(BF16)
    row = lax.broadcasted_iota(I32, (_sorted_rows(td), td), 0)
    pick = jnp.logical_or(row == ei_ref[2:3, :], row == ei_ref[3:4, :])
    srt_ref[slot] = jnp.dot(jnp.where(pick, 1.0, 0.0).astype(BF16), hn, preferred_element_type=F32)

    def copy(ring_slot, src_row, dst_row, rows):
        return pltpu.make_async_copy(srt_ref.at[ring_slot, pl.ds(src_row, rows)],
                                     xs_ref.at[pl.ds(dst_row, rows)], sem.at[ring_slot])

    def for_pieces(of_step, ring_slot, act):
        for present, src_row, dst_row, rows in _segment_pieces(of_step, slen_ref, soff_ref, sdst_ref, td):
            @pl.when(present)
            def _(src_row=src_row, dst_row=dst_row, rows=rows):
                act(copy(ring_slot, src_row, dst_row, rows))

    for_pieces(step, slot, lambda c: c.start())

    @pl.when(step > 0)
    def _():
        for_pieces(step - 1, 1 - slot, lambda c: c.wait())

    @pl.when(step == last)
    def _():
        for_pieces(step, slot, lambda c: c.wait())

    @pl.when(step == 0)
    def _():
        for present, dst_row, rows in _pad_pieces(pad_ref):
            @pl.when(present)
            def _(dst_row=dst_row, rows=rows):
                pad_copy(dst_row, rows).wait()

        def wait_tail(j, carry):
            tail_copy(j).wait()
            return carry

        lax.fori_loop(0, pad_ref[2 * N_EXPERTS + 1], wait_tail, 0)


def _stage_dispatch(seg_len, seg_off, seg_dst, pads, h, norm_w, ei, n_slots):
    t, d = h.shape
    td = _tile(t, ROW_TILE)
    return pl.pallas_call(
        functools.partial(_dispatch_kernel, td=td),
        grid_spec=pltpu.PrefetchScalarGridSpec(
            num_scalar_prefetch=4, grid=(t // td,),
            in_specs=[pl.BlockSpec((td, d), lambda i, *_: (i, 0)),
                      pl.BlockSpec((1, d), lambda i, *_: (0, 0)),
                      pl.BlockSpec((8, td), lambda i, *_: (0, i))],
            out_specs=pl.BlockSpec(memory_space=pl.ANY),
            scratch_shapes=[pltpu.VMEM((2, _sorted_rows(td), d), F32), pltpu.VMEM((MOE_ROWS, d), F32),
                            pltpu.SemaphoreType.DMA((2,)), pltpu.SemaphoreType.DMA]),
        out_shape=jax.ShapeDtypeStruct((n_slots, d), F32),
        compiler_params=pltpu.CompilerParams(dimension_semantics=("arbitrary",), has_side_effects=True,
                                             vmem_limit_bytes=VMEM_LIMIT),
        name="moe_dispatch",
    )(seg_len, seg_off, seg_dst, pads, h, norm_w.reshape(1, d), ei)


def _moe_kernel(be_ref, nu_ref, xs_ref, wg_ref, wu_ref, wd_ref, ys_ref, xb_ref, hd_ref, *, tf):
    del be_ref
    f = pl.program_id(1)

    @pl.when(pl.program_id(0) < nu_ref[0])
    def _():
        @pl.when(f == 0)
        def _():
            xb_ref[...] = xs_ref[...].astype(BF16)

        x = xb_ref[...]
        for c in range(tf // MOE_SUB):
            cs = slice(c * MOE_SUB, (c + 1) * MOE_SUB)
            g = jnp.dot(x, wg_ref[0, :, cs], preferred_element_type=F32)
            u = jnp.dot(x, wu_ref[0, :, cs], preferred_element_type=F32)
            hd_ref[:, cs] = (_silu(g) * u).astype(BF16)
        y = jnp.dot(hd_ref[...], wd_ref[0], preferred_element_type=F32)

        @pl.when(f == 0)
        def _():
            ys_ref[...] = y

        @pl.when(f > 0)
        def _():
            ys_ref[...] += y

    @pl.when(jnp.logical_and(pl.program_id(0) >= nu_ref[0], f == 0))
    def _():
        ys_ref[...] = jnp.zeros_like(ys_ref)


def _stage_moe(blk_e, n_used, xs, w_gate, w_up, w_down):
    n_slots, d = xs.shape
    d_ff = w_gate.shape[-1]
    tm = MOE_ROWS
    tf = _tile(d_ff, MOE_COLS)
    nb, nf = n_slots // tm, d_ff // tf

    def blk(b, nu):
        return jnp.minimum(b, nu[0] - 1)

    def col(b, f, nu):
        return jnp.where(b < nu[0], f, nf - 1)

    return pl.pallas_call(
        functools.partial(_moe_kernel, tf=tf),
        grid_spec=pltpu.PrefetchScalarGridSpec(
            num_scalar_prefetch=2, grid=(nb, nf),
            in_specs=[pl.BlockSpec((tm, d), lambda b, f, be, nu: (blk(b, nu), 0)),
                      pl.BlockSpec((1, d, tf), lambda b, f, be, nu: (be[blk(b, nu)], 0, col(b, f, nu))),
                      pl.BlockSpec((1, d, tf), lambda b, f, be, nu: (be[blk(b, nu)], 0, col(b, f, nu))),
                      pl.BlockSpec((1, tf, d), lambda b, f, be, nu: (be[blk(b, nu)], col(b, f, nu), 0))],
            out_specs=pl.BlockSpec((tm, d), lambda b, f, be, nu: (b, 0)),
            scratch_shapes=[pltpu.VMEM((tm, d), BF16), pltpu.VMEM((tm, tf), BF16)]),
        out_shape=jax.ShapeDtypeStruct((n_slots, d), F32),
        compiler_params=_params(("arbitrary", "arbitrary")),
        name="moe_experts",
    )(blk_e, n_used, xs, w_gate.astype(BF16), w_up.astype(BF16), w_down.astype(BF16))


def _combine_kernel(slen_ref, soff_ref, sdst_ref, h_ref, w_ref, nw_ref, ys_ref, o_ref, buf_ref, sem,
                    *, tc):
    i = pl.program_id(0)
    n = pl.num_programs(0)

    def copy(step_slot, src_row, dst_row, rows):
        return pltpu.make_async_copy(ys_ref.at[pl.ds(src_row, rows)],
                                     buf_ref.at[step_slot, pl.ds(dst_row, rows)], sem.at[step_slot])

    def gather(step, step_slot):
        for present, off, dst, rows in _segment_pieces(step, slen_ref, soff_ref, sdst_ref, tc):
            @pl.when(present)
            def _(off=off, dst=dst, rows=rows):
                copy(step_slot, dst, off, rows).start()

    @pl.when(i == 0)
    def _():
        buf_ref[:, 2 * tc:, :] = jnp.zeros((2, _sorted_rows(tc) - 2 * tc, buf_ref.shape[-1]), F32)
        gather(0, 0)

    slot = lax.rem(i, 2)

    @pl.when(i + 1 < n)
    def _():
        gather(i + 1, 1 - slot)

    for present, off, dst, rows in _segment_pieces(i, slen_ref, soff_ref, sdst_ref, tc):
        @pl.when(present)
        def _(off=off, dst=dst, rows=rows):
            copy(slot, dst, off, rows).wait()

    yb = buf_ref[slot].astype(BF16)
    pos = lax.broadcasted_iota(I32, (tc, _sorted_rows(tc)), 1)
    sel0 = jnp.where(pos == w_ref[:, 2:3].astype(I32), 1.0, 0.0).astype(BF16)
    sel1 = jnp.where(pos == w_ref[:, 3:4].astype(I32), 1.0, 0.0).astype(BF16)
    y0 = jnp.dot(sel0, yb, preferred_element_type=F32)
    y1 = jnp.dot(sel1, yb, preferred_element_type=F32)
    h = h_ref[...] + (y0 * w_ref[:, 0:1] + y1 * w_ref[:, 1:2])
    o_ref[...] = _rms(h, nw_ref[...])


def _stage_combine(seg_len, seg_off, seg_dst, h, wcol, norm_w, ys):
    t, d = h.shape
    tc = _tile(t, ROW_TILE)
    return pl.pallas_call(
        functools.partial(_combine_kernel, tc=tc),
        grid_spec=pltpu.PrefetchScalarGridSpec(
            num_scalar_prefetch=3, grid=(t // tc,),
            in_specs=[pl.BlockSpec((tc, d), lambda i, *_: (i, 0)),
                      pl.BlockSpec((tc, LANES), lambda i, *_: (i, 0)),
                      pl.BlockSpec((1, d), lambda i, *_: (0, 0)),
                      pl.BlockSpec(memory_space=pl.ANY)],
            out_specs=pl.BlockSpec((tc, d), lambda i, *_: (i, 0)),
            scratch_shapes=[pltpu.VMEM((2, _sorted_rows(tc), d), F32), pltpu.SemaphoreType.DMA((2,))]),
        out_shape=jax.ShapeDtypeStruct((t, d), F32),
        compiler_params=_params(("arbitrary",)),
        name="moe_combine",
    )(seg_len, seg_off, seg_dst, h, wcol, norm_w.reshape(1, d), ys)


def kernel(x, attn_norm_w, ffn_norm_w, a_w_in, a_conv_w, a_a_log, a_dt_bias, a_onorm_w, a_w_out,
           kv_norm_w, w_kv, b_w_q, b_sinks, b_w_o, rel_bias, ffn_w_gate, ffn_w_up, ffn_w_down,
           moe_router, moe_w_gate, moe_w_up, moe_w_down, final_norm_w):
    bsz, L, d = x.shape
    t = bsz * L

    q, k, v, gate, bg, gct = _stage_in(x, attn_norm_w[0], a_w_in[0], a_conv_w[0], a_a_log[0],
                                       a_dt_bias[0])
    og = _stage_delta(q, k, v, gate, bg, gct, a_onorm_w[0])
    h, q2, kv2 = _stage_ffn(og.reshape(t, A_WIDTH), x.reshape(t, d), a_w_out[0], ffn_norm_w[0],
                            ffn_w_gate[0], ffn_w_up[0], ffn_w_down[0], attn_norm_w[1], kv_norm_w,
                            b_w_q[0], w_kv)

    bias4 = _stage_bias(rel_bias)
    h, ei, wcol, cnt, stat = _stage_attn(q2, kv2, bias4, b_sinks[0], b_w_o[0], h.reshape(bsz, L, d),
                                         ffn_norm_w[1], moe_router[0])
    h = h.reshape(t, d)

    counts = cnt[:N_EXPERTS, 0].astype(I32)
    p_counts = (counts + MOE_ROWS - 1) // MOE_ROWS * MOE_ROWS
    p_end = jnp.cumsum(p_counts)
    p_start = p_end - p_counts
    tile_cnt = stat[:, :N_EXPERTS, 0].astype(I32)
    before = stat[:, :N_EXPERTS, 1].astype(I32)
    seg_len = tile_cnt.reshape(-1)
    seg_off = (jnp.cumsum(tile_cnt, axis=1) - tile_cnt).reshape(-1)
    seg_dst = (p_start[None, :] + before).reshape(-1)
    n_tiles = stat.shape[0]
    n_blocks = -(-(2 * t + n_tiles * N_EXPERTS * (SEG_ALIGN - 1)) // MOE_ROWS) + N_EXPERTS
    blk_e = jnp.minimum(jnp.searchsorted(p_end, jnp.arange(n_blocks, dtype=I32) * MOE_ROWS, side='right'),
                        N_EXPERTS - 1).astype(I32)
    n_used = (p_end[-1:] // MOE_ROWS).astype(I32)
    pad_start = p_start + counts
    pads = jnp.concatenate([jnp.stack([pad_start, p_end - pad_start], axis=1).reshape(-1),
                            p_end[-1:], n_blocks - n_used]).astype(I32)
    xs = _stage_dispatch(seg_len, seg_off, seg_dst, pads, h, ffn_norm_w[1], ei, n_blocks * MOE_ROWS)
    ys = _stage_moe(blk_e, n_used, xs, moe_w_gate[0], moe_w_up[0], moe_w_down[0])
    out = _stage_combine(seg_len, seg_off, seg_dst, h, wcol, final_norm_w, ys)
    return out.reshape(bsz, L, d)
```

```python
import functools
import math

import numpy as np
import jax
import jax.numpy as jnp
from jax import lax
from jax.experimental import pallas as pl
from jax.experimental.pallas import tpu as pltpu

F32 = jnp.float32
BF16 = jnp.bfloat16
I32 = jnp.int32

EPS = 1e-6
CHUNK = 64
A_HEADS = 8
A_HEAD_DIM = 128
A_WIDTH = A_HEADS * A_HEAD_DIM
A_CONV = 4
B_Q_HEADS = 16
B_KV_HEADS = 4
B_HEAD_DIM = 64
B_GROUP = B_Q_HEADS // B_KV_HEADS
WINDOW_CHUNKS = 2
BAND = (WINDOW_CHUNKS + 1) * CHUNK
REL_BUCKETS = 32
REL_MAX_DIST = 128
N_EXPERTS = 8
LANES = 128

VMEM_LIMIT = 56 * 1024 * 1024

DELTA_TILE = 1024
ROW_TILE = 512
CONV_COLS = 512
FFN_COLS = 256
MOE_ROWS = 1024
MOE_COLS = 1792
SEG_ALIGN = 8
MOE_SUB = 256
ATT_Q = 2 * CHUNK
LOCAL_CHUNKS = 2


def _tile(n, pref):
    t = min(n, pref)
    assert n % t == 0, (n, t)
    return t


def _mm(a, b):
    return jnp.dot(a.astype(BF16), b.astype(BF16), preferred_element_type=F32)


def _mm_nt(a, b):
    return lax.dot_general(a.astype(BF16), b.astype(BF16), (((1,), (1,)), ((), ())),
                           preferred_element_type=F32)


def _mm_tn(a, b):
    return lax.dot_general(a.astype(BF16), b.astype(BF16), (((0,), (0,)), ((), ())),
                           preferred_element_type=F32)


def _silu(x):
    return x * jax.nn.sigmoid(x)


def _rms(x, w):
    return x * lax.rsqrt(jnp.mean(x * x, axis=-1, keepdims=True) + EPS) * w


def _const_spec(shape):
    nd = len(shape)
    return pl.BlockSpec(shape, lambda *_: (0,) * nd, pipeline_mode=pl.Buffered(1))


def _params(sem):
    return pltpu.CompilerParams(dimension_semantics=sem, vmem_limit_bytes=VMEM_LIMIT)


def _in_kernel(x_ref, nw_ref, w_ref, wba_ref, cw_ref, ap_ref, seg_ref,
               q_ref, k_ref, v_ref, gate_ref, bg_ref, gct_ref, ext_ref, *, tm):
    @pl.when(pl.program_id(1) == 0)
    def _():
        ext_ref[0:8, :] = jnp.zeros((8, 3 * A_WIDTH), F32)

    hn = _rms(x_ref[0], nw_ref[...]).astype(BF16)

    ba = jnp.dot(hn, wba_ref[...], preferred_element_type=F32)
    beta = jax.nn.sigmoid(ba)
    z = ba + ap_ref[1:2, :]
    softplus = jnp.maximum(z, 0.0) + jnp.log1p(jnp.exp(-jnp.abs(z)))
    g = -jnp.exp(ap_ref[0:1, :]) * softplus
    g1 = g.astype(BF16)
    r1 = g - g1.astype(F32)
    g2 = r1.astype(BF16)
    g3 = (r1 - g2.astype(F32)).astype(BF16)
    seg = seg_ref[...]
    gc = (jnp.dot(seg, g1, preferred_element_type=F32) + jnp.dot(seg, g2, preferred_element_type=F32)
          + jnp.dot(seg, g3, preferred_element_type=F32))
    lane = lax.broadcasted_iota(I32, (tm, LANES), 1)
    bg = jnp.where(lane < A_HEADS, beta, gc)
    bg_ref[0] = bg
    bgt = bg.T
    for ci in range(tm // CHUNK):
        gct_ref[0, ci] = bgt[0:2 * A_HEADS, ci * CHUNK:(ci + 1) * CHUNK]

    outs = (q_ref, k_ref, v_ref)
    for c in range(3 * A_WIDTH // CONV_COLS):
        cs = slice(c * CONV_COLS, (c + 1) * CONV_COLS)
        p = jnp.dot(hn, w_ref[:, cs], preferred_element_type=F32)
        ext_ref[8:tm + 8, cs] = p
        acc = cw_ref[A_CONV - 1:A_CONV, cs] * p
        for j in range(A_CONV - 1):
            acc = acc + cw_ref[j:j + 1, cs] * ext_ref[5 + j:5 + j + tm, cs]
        ext_ref[0:8, cs] = p[tm - 8:, :]
        a = _silu(acc)
        which, off = divmod(c * CONV_COLS, A_WIDTH)
        for hh in range(CONV_COLS // A_HEAD_DIM):
            hs = a[:, hh * A_HEAD_DIM:(hh + 1) * A_HEAD_DIM]
            if which < 2:
                scale = lax.rsqrt(jnp.sum(hs * hs, axis=-1, keepdims=True) + EPS)
                hs = hs * (scale * (A_HEAD_DIM ** -0.5) if which == 0 else scale)
            lo = off + hh * A_HEAD_DIM
            outs[which][0, :, lo:lo + A_HEAD_DIM] = hs.astype(BF16)

    for c in range(A_WIDTH // CONV_COLS):
        cs = slice(3 * A_WIDTH + c * CONV_COLS, 3 * A_WIDTH + (c + 1) * CONV_COLS)
        gate_ref[0, :, c * CONV_COLS:(c + 1) * CONV_COLS] = jnp.dot(
            hn, w_ref[:, cs], preferred_element_type=F32)


def _stage_in(x, norm_w, w_in, conv_w, a_log, dt_bias):
    bsz, L, d = x.shape
    tm = _tile(L, ROW_TILE)
    nc = L // CHUNK
    w_main = w_in[:, :4 * A_WIDTH].astype(BF16)
    w_ba = jnp.pad(w_in[:, 4 * A_WIDTH:], ((0, 0), (0, LANES - 2 * A_HEADS))).astype(BF16)
    ap = jnp.zeros((2, LANES), F32)
    ap = ap.at[0, A_HEADS:2 * A_HEADS].set(a_log.astype(F32))
    ap = ap.at[1, A_HEADS:2 * A_HEADS].set(dt_bias.astype(F32))
    r = np.arange(tm)
    seg = jnp.asarray(((r[:, None] >= r[None, :]) &
                       (r[:, None] // CHUNK == r[None, :] // CHUNK)).astype(np.float32)).astype(BF16)
    row_spec = lambda w: pl.BlockSpec((1, tm, w), lambda b, i: (b, i, 0))
    act = jax.ShapeDtypeStruct((bsz, L, A_WIDTH), BF16)
    return pl.pallas_call(
        functools.partial(_in_kernel, tm=tm),
        grid=(bsz, L // tm),
        in_specs=[row_spec(d), _const_spec((1, d)), _const_spec((d, 4 * A_WIDTH)),
                  _const_spec((d, LANES)), _const_spec((A_CONV, 3 * A_WIDTH)),
                  _const_spec((2, LANES)), _const_spec((tm, tm))],
        out_specs=[row_spec(A_WIDTH), row_spec(A_WIDTH), row_spec(A_WIDTH), row_spec(A_WIDTH),
                   row_spec(LANES),
                   pl.BlockSpec((1, tm // CHUNK, 2 * A_HEADS, CHUNK), lambda b, i: (b, i, 0, 0))],
        out_shape=[act, act, act, jax.ShapeDtypeStruct((bsz, L, A_WIDTH), F32),
                   jax.ShapeDtypeStruct((bsz, L, LANES), F32),
                   jax.ShapeDtypeStruct((bsz, nc, 2 * A_HEADS, CHUNK), F32)],
        scratch_shapes=[pltpu.VMEM((tm + 8, 3 * A_WIDTH), F32)],
        compiler_params=_params(("arbitrary", "arbitrary")),
        name="in_proj",
    )(x, norm_w.reshape(1, d), w_main, w_ba, conv_w.astype(F32), ap, seg)


def _chunk_rows(c):
    if isinstance(c, int):
        return slice(c * CHUNK, (c + 1) * CHUNK)
    return pl.ds(pl.multiple_of(c * CHUNK, CHUNK), CHUNK)


def _delta_kernel(q_ref, k_ref, v_ref, gate_ref, bg_ref, gct_ref, ow_ref, o_ref,
                  s_ref, u_ref, w_ref, qd_ref, kd_ref, at_ref, *, cb):
    @pl.when(pl.program_id(1) == 0)
    def _():
        s_ref[...] = jnp.zeros_like(s_ref)

    heads = range(A_HEADS)
    hcols = [slice(h * A_HEAD_DIM, (h + 1) * A_HEAD_DIM) for h in heads]
    row = lax.broadcasted_iota(I32, (CHUNK, CHUNK), 0)
    col = lax.broadcasted_iota(I32, (CHUNK, CHUNK), 1)
    tri = row >= col
    strict = row > col
    diag = (row >> 4) == (col >> 4)
    eye = (row == col).astype(F32)

    def local(cc):
        items = [(j, h) for j in range(LOCAL_CHUNKS) for h in heads]
        its = range(len(items))
        cidx = [cc * LOCAL_CHUNKS + j for j in range(LOCAL_CHUNKS)]
        crow = [_chunk_rows(c) for c in cidx]
        rows = [crow[j] for j, _ in items]
        cols = [hcols[h] for _, h in items]
        qb = [q_ref[0, rows[i], cols[i]] for i in its]
        kb = [k_ref[0, rows[i], cols[i]] for i in its]
        vb = [v_ref[0, rows[i], cols[i]] for i in its]
        bgs = [bg_ref[0, crow[j], :] for j in range(LOCAL_CHUNKS)]
        gts = [gct_ref[0, c] for c in cidx]
        beta = [bgs[j][:, h:h + 1] for j, h in items]
        gcol = [bgs[j][:, A_HEADS + h:A_HEADS + h + 1] for j, h in items]
        grow = [gts[j][A_HEADS + h:A_HEADS + h + 1, :] for j, h in items]

        decay = [jnp.exp(jnp.where(tri, gcol[i] - grow[i], -jnp.inf)) for i in its]
        eg = [jnp.exp(gcol[i]) for i in its]
        kf = [kb[i].astype(F32) for i in its]
        kbeta = [kf[i] * beta[i] for i in its]
        kq = [_mm_nt(jnp.concatenate([kbeta[i].astype(BF16), qb[i]], axis=0), kb[i]) for i in its]
        lower = [jnp.where(strict, kq[i][:CHUNK] * decay[i], 0.0) for i in its]
        ld = [jnp.where(diag, lower[i], 0.0) for i in its]
        lo = [lower[i] - ld[i] for i in its]
        p = [eye - ld[i] for i in its]
        s = [_mm(ld[i], ld[i]) for i in its]
        for _ in range(2):
            ps = [_mm(jnp.concatenate([p[i], s[i]], axis=0), s[i]) for i in its]
            p = [p[i] + ps[i][:CHUNK] for i in its]
            s = [ps[i][CHUNK:] for i in its]
        td = [p[i] + _mm(p[i], s[i]) for i in its]
        rhs = [jnp.concatenate([vb[i].astype(F32) * beta[i], kbeta[i] * eg[i]], axis=1) for i in its]
        r = [_mm(td[i], rhs[i]) for i in its]
        m = [_mm(td[i], lo[i]) for i in its]
        m2 = [_mm(m[i], m[i]) for i in its]
        r = [r[i] + _mm(m2[i], r[i]) for i in its]
        uw = [r[i] - _mm(m[i], r[i]) for i in its]
        attn = [kq[i][CHUNK:] * decay[i] for i in its]
        qd = [qb[i].astype(F32) * eg[i] for i in its]
        kd = [kf[i] * jnp.exp(grow[i][:, CHUNK - 1:CHUNK] - gcol[i]) for i in its]

        def store():
            for i, (_, h) in enumerate(items):
                u_ref[rows[i], cols[i]] = uw[i][:, :A_HEAD_DIM]
                w_ref[rows[i], cols[i]] = uw[i][:, A_HEAD_DIM:].astype(BF16)
                qd_ref[rows[i], cols[i]] = qd[i].astype(BF16)
                kd_ref[rows[i], cols[i]] = kd[i].astype(BF16)
                at_ref[h, rows[i], :] = attn[i].astype(BF16)

        return store

    def recur(cc):
        cidx = [cc * LOCAL_CHUNKS + j for j in range(LOCAL_CHUNKS)]
        crow = [_chunk_rows(c) for c in cidx]
        wq = [[jnp.concatenate([w_ref[r, hcols[h]], qd_ref[r, hcols[h]]], axis=0) for h in heads]
              for r in crow]
        u = [[u_ref[r, hcols[h]] for h in heads] for r in crow]
        kd = [[kd_ref[r, hcols[h]] for h in heads] for r in crow]
        at = [[at_ref[h, r, :] for h in heads] for r in crow]
        gate = [[gate_ref[0, r, hcols[h]] for h in heads] for r in crow]
        gts = [gct_ref[0, c] for c in cidx]
        st = [s_ref[h] for h in heads]
        ow = ow_ref[...]

        og = []
        for j in range(LOCAL_CHUNKS):
            eglast = [jnp.exp(gts[j][A_HEADS + h:A_HEADS + h + 1, CHUNK - 1:CHUNK]) for h in heads]
            ws_qs = [_mm(wq[j][h], st[h]) for h in heads]
            vnew = [u[j][h] - ws_qs[h][:CHUNK] for h in heads]
            av = [_mm(jnp.concatenate([at[j][h], kd[j][h].T], axis=0), vnew[h]) for h in heads]
            o = [ws_qs[h][CHUNK:] + av[h][:CHUNK] for h in heads]
            st = [st[h] * eglast[h] + av[h][CHUNK:] for h in heads]
            og.append([_rms(o[h], ow) * _silu(gate[j][h]) for h in heads])

        def store():
            for h in heads:
                s_ref[h] = st[h]
                for j in range(LOCAL_CHUNKS):
                    o_ref[0, crow[j], hcols[h]] = og[j][h].astype(BF16)

        return store

    def fused(cc, carry):
        store_local = local(cc + 1)
        store_recur = recur(cc)
        store_local()
        store_recur()
        return carry

    groups = cb // LOCAL_CHUNKS
    local(0)()
    lax.fori_loop(0, groups - 1, fused, 0)
    recur(groups - 1)()


def _stage_delta(q, k, v, gate, bg, gct, onorm_w):
    bsz, L, _ = q.shape
    tm = _tile(L, DELTA_TILE)
    cb = tm // CHUNK
    row_spec = lambda w: pl.BlockSpec((1, tm, w), lambda b, i: (b, i, 0))
    return pl.pallas_call(
        functools.partial(_delta_kernel, cb=cb),
        grid=(bsz, L // tm),
        in_specs=[row_spec(A_WIDTH), row_spec(A_WIDTH), row_spec(A_WIDTH), row_spec(A_WIDTH),
                  row_spec(LANES),
                  pl.BlockSpec((1, cb, 2 * A_HEADS, CHUNK), lambda b, i: (b, i, 0, 0)),
                  _const_spec((1, A_HEAD_DIM))],
        out_specs=row_spec(A_WIDTH),
        out_shape=jax.ShapeDtypeStruct((bsz, L, A_WIDTH), BF16),
        scratch_shapes=[pltpu.VMEM((A_HEADS, A_HEAD_DIM, A_HEAD_DIM), F32),
                        pltpu.VMEM((tm, A_WIDTH), F32), pltpu.VMEM((tm, A_WIDTH), BF16),
                        pltpu.VMEM((tm, A_WIDTH), BF16), pltpu.VMEM((tm, A_WIDTH), BF16),
                        pltpu.VMEM((A_HEADS, tm, CHUNK), BF16)],
        compiler_params=_params(("arbitrary", "arbitrary")),
        name="delta_rule",
    )(q, k, v, gate, bg, gct, onorm_w.reshape(1, A_HEAD_DIM).astype(F32))


def _ffn_kernel(og_ref, x_ref, wo_ref, fw_ref, wg_ref, wu_ref, wd_ref, aw_ref, kw_ref, wq_ref,
                wkv_ref, h_ref, q_ref, kv_ref, hd_ref, *, d_ff):
    h1 = x_ref[...] + jnp.dot(og_ref[...], wo_ref[...], preferred_element_type=F32)
    hn = _rms(h1, fw_ref[...]).astype(BF16)
    for c in range(d_ff // FFN_COLS):
        cs = slice(c * FFN_COLS, (c + 1) * FFN_COLS)
        g = jnp.dot(hn, wg_ref[:, cs], preferred_element_type=F32)
        u = jnp.dot(hn, wu_ref[:, cs], preferred_element_type=F32)
        hd_ref[:, cs] = (_silu(g) * u).astype(BF16)
    h2 = h1 + jnp.dot(hd_ref[...], wd_ref[...], preferred_element_type=F32)
    h_ref[...] = h2
    xh = h2 * lax.rsqrt(jnp.mean(h2 * h2, axis=-1, keepdims=True) + EPS)
    q = jnp.dot((xh * aw_ref[...]).astype(BF16), wq_ref[...], preferred_element_type=F32)
    q_ref[...] = (q * (B_HEAD_DIM ** -0.5)).astype(BF16)
    kv_ref[...] = jnp.dot((xh * kw_ref[...]).astype(BF16), wkv_ref[...],
                          preferred_element_type=F32).astype(BF16)


def _stage_ffn(og, x, w_out, ffn_norm_w, w_gate, w_up, w_down, attn_norm_w, kv_norm_w, w_q, w_kv):
    t, d = x.shape
    d_ff = w_gate.shape[1]
    tm = _tile(t, ROW_TILE)
    hq = B_Q_HEADS * B_HEAD_DIM
    half = B_KV_HEADS * B_HEAD_DIM
    dup = lambda w: jnp.concatenate([w.reshape(d, B_KV_HEADS, 1, B_HEAD_DIM)] * 2, axis=2).reshape(d, 2 * half)
    w_kvd = jnp.concatenate([dup(w_kv[:, :half]), dup(w_kv[:, half:])], axis=1).astype(BF16)
    row_spec = lambda w: pl.BlockSpec((tm, w), lambda i: (i, 0))
    return pl.pallas_call(
        functools.partial(_ffn_kernel, d_ff=d_ff),
        grid=(t // tm,),
        in_specs=[row_spec(A_WIDTH), row_spec(d), _const_spec((A_WIDTH, d)), _const_spec((1, d)),
                  _const_spec((d, d_ff)), _const_spec((d, d_ff)), _const_spec((d_ff, d)),
                  _const_spec((1, d)), _const_spec((1, d)), _const_spec((d, hq)),
                  _const_spec((d, 4 * half))],
        out_specs=[row_spec(d), row_spec(hq), row_spec(4 * half)],
        out_shape=[jax.ShapeDtypeStruct((t, d), F32), jax.ShapeDtypeStruct((t, hq), BF16),
                   jax.ShapeDtypeStruct((t, 4 * half), BF16)],
        scratch_shapes=[pltpu.VMEM((tm, d_ff), BF16)],
        compiler_params=_params(("arbitrary",)),
        name="ffn_dense",
    )(og, x, w_out.astype(BF16), ffn_norm_w.reshape(1, d), w_gate.astype(BF16), w_up.astype(BF16),
      w_down.astype(BF16), attn_norm_w.reshape(1, d), kv_norm_w.reshape(1, d), w_q.astype(BF16), w_kvd)


def _rel_buckets(rel):
    nb = REL_BUCKETS // 2
    max_exact = nb // 2
    ret = jnp.where(rel > 0, nb, 0)
    dist = jnp.abs(rel)
    dist_f = jnp.maximum(dist, 1).astype(F32)
    large = max_exact + (jnp.log(dist_f / max_exact) / math.log(REL_MAX_DIST / max_exact)
                         * (nb - max_exact)).astype(I32)
    large = jnp.minimum(large, nb - 1)
    return ret + jnp.where(dist < max_exact, dist, large)


def _bias_kernel(rb_ref, bk_ref, o_ref):
    bk = bk_ref[...]
    for h in range(B_Q_HEADS):
        acc = jnp.zeros(bk.shape, F32)
        for b in range(REL_BUCKETS):
            acc = jnp.where(bk == b, rb_ref[b * B_Q_HEADS + h], acc)
        o_ref[h] = acc


def _stage_bias(rel_bias):
    rel = (jnp.arange(BAND)[None, :] - WINDOW_CHUNKS * CHUNK) - jnp.arange(CHUNK)[:, None]
    buckets = _rel_buckets(rel).astype(I32)
    bias = pl.pallas_call(
        _bias_kernel,
        in_specs=[pl.BlockSpec(memory_space=pltpu.SMEM), pl.BlockSpec(memory_space=pltpu.VMEM)],
        out_specs=pl.BlockSpec(memory_space=pltpu.VMEM),
        out_shape=jax.ShapeDtypeStruct((B_Q_HEADS, CHUNK, BAND), F32),
        name="rel_bias",
    )(rel_bias.astype(F32).reshape(-1), buckets)
    pad = lambda lo, hi: jnp.pad(bias, ((0, 0), (0, 0), (lo, hi)), constant_values=-jnp.inf)
    pair = jnp.concatenate([pad(0, CHUNK), pad(CHUNK, 0)], axis=1)
    return pair.reshape(B_KV_HEADS, B_GROUP * ATT_Q, 2 * ATT_Q)


def _attn_kernel(q_ref, kv_ref, kvp_ref, bias_ref, sink_ref, wo_ref, h_ref, nw_ref, rw_ref, tri_ref,
                 o_ref, ei_ref, wcol_ref, cnt_ref, stat_ref, att_ref, carry_ref, *, nblk, tq):
    first = pl.program_id(1) == 0
    lane = lax.broadcasted_iota(I32, (ATT_Q, LANES), 1)
    lo_half = lane < B_HEAD_DIM
    kidx = lax.broadcasted_iota(I32, (B_GROUP * ATT_Q, 2 * ATT_Q), 1)
    koff = B_KV_HEADS * LANES
    for jb in range(nblk):
        rows = slice(jb * ATT_Q, (jb + 1) * ATT_Q)
        if jb == 0:
            keys = jnp.concatenate([kvp_ref[0], kv_ref[0, :ATT_Q, :]], axis=0)
        else:
            keys = kv_ref[0, (jb - 1) * ATT_Q:(jb + 1) * ATT_Q, :]
        for hk in range(B_KV_HEADS):
            kd = keys[:, hk * LANES:(hk + 1) * LANES]
            vd = keys[:, koff + hk * LANES:koff + (hk + 1) * LANES]
            qa = q_ref[0, rows, (2 * hk) * LANES:(2 * hk + 1) * LANES]
            qb = q_ref[0, rows, (2 * hk + 1) * LANES:(2 * hk + 2) * LANES]
            zero = jnp.zeros_like(qa)
            q4 = jnp.concatenate([jnp.where(lo_half, qa, zero), jnp.where(lo_half, zero, qa),
                                  jnp.where(lo_half, qb, zero), jnp.where(lo_half, zero, qb)], axis=0)
            s = lax.dot_general(q4, kd, (((1,), (1,)), ((), ())), preferred_element_type=F32)
            s = s + bias_ref[hk]
            if jb == 0:
                s = jnp.where(jnp.logical_and(first, kidx < ATT_Q), -jnp.inf, s)
            sink = sink_ref[hk]
            m = jnp.maximum(jnp.max(s, axis=-1, keepdims=True), sink)
            p = jnp.exp(s - m)
            denom = jnp.sum(p, axis=-1, keepdims=True) + jnp.exp(sink - m)
            o = jnp.dot(p.astype(BF16), vd, preferred_element_type=F32) / denom
            ta = jnp.where(lo_half, o[0:ATT_Q], o[ATT_Q:2 * ATT_Q])
            tb = jnp.where(lo_half, o[2 * ATT_Q:3 * ATT_Q], o[3 * ATT_Q:4 * ATT_Q])
            att_ref[rows, (2 * hk) * LANES:(2 * hk + 1) * LANES] = ta.astype(BF16)
            att_ref[rows, (2 * hk + 1) * LANES:(2 * hk + 2) * LANES] = tb.astype(BF16)
    h_out = h_ref[0] + jnp.dot(att_ref[...], wo_ref[...], preferred_element_type=F32)
    o_ref[0] = h_out
    _route(h_out, jnp.logical_and(pl.program_id(0) == 0, first), nw_ref, rw_ref, tri_ref,
           ei_ref, wcol_ref, cnt_ref, stat_ref, carry_ref, tq)


def _stage_attn(q, kv, bias4, sinks, w_o, h, ffn_norm_w, router_w):
    bsz, L, d = h.shape
    tq = _tile(L, ROW_TILE)
    nt = L // tq
    halo = WINDOW_CHUNKS * CHUNK
    per = tq // halo
    hq = B_Q_HEADS * B_HEAD_DIM
    sink4 = jnp.repeat(sinks.astype(F32).reshape(B_KV_HEADS, B_GROUP), ATT_Q, axis=1)[..., None]
    rows = 2 * N_EXPERTS
    rwt = jnp.pad(router_w.T, ((0, rows - N_EXPERTS), (0, 0))).astype(BF16)
    r = np.arange(tq)
    tri = jnp.asarray((r[:, None] < r[None, :]).astype(np.float32)).astype(BF16)
    row_spec = lambda w: pl.BlockSpec((1, tq, w), lambda b, i: (b, i, 0))
    return pl.pallas_call(
        functools.partial(_attn_kernel, nblk=tq // ATT_Q, tq=tq),
        grid=(bsz, nt),
        in_specs=[row_spec(hq), row_spec(kv.shape[-1]),
                  pl.BlockSpec((1, halo, kv.shape[-1]), lambda b, i: (b, jnp.maximum(i * per - 1, 0), 0)),
                  _const_spec(bias4.shape), _const_spec(sink4.shape), _const_spec((hq, d)),
                  row_spec(d), _const_spec((1, d)), _const_spec((rows, d)), _const_spec((tq, tq))],
        out_specs=[row_spec(d),
                   pl.BlockSpec((8, tq), lambda b, i: (0, b * nt + i)),
                   pl.BlockSpec((tq, LANES), lambda b, i: (b * nt + i, 0)),
                   pl.BlockSpec((rows, LANES), lambda b, i: (0, 0)),
                   pl.BlockSpec((1, rows, LANES), lambda b, i: (b * nt + i, 0, 0))],
        out_shape=[jax.ShapeDtypeStruct((bsz, L, d), F32),
                   jax.ShapeDtypeStruct((8, bsz * L), I32),
                   jax.ShapeDtypeStruct((bsz * L, LANES), F32),
                   jax.ShapeDtypeStruct((rows, LANES), F32),
                   jax.ShapeDtypeStruct((bsz * nt, rows, LANES), F32)],
        scratch_shapes=[pltpu.VMEM((tq, hq), BF16), pltpu.VMEM((rows, LANES), F32)],
        compiler_params=_params(("arbitrary", "arbitrary")),
        name="swa_attn",
    )(q.reshape(bsz, L, hq), kv.reshape(bsz, L, -1), kv.reshape(bsz, L, -1), bias4, sink4,
      w_o.astype(BF16), h, ffn_norm_w.reshape(1, d), rwt, tri)


def _route(h, first, nw_ref, rw_ref, tri_ref, ei_ref, wcol_ref, cnt_ref, stat_ref, carry_ref, tr):
    @pl.when(first)
    def _():
        carry_ref[...] = jnp.zeros_like(carry_ref)

    hn = _rms(h, nw_ref[...])
    rows = 2 * N_EXPERTS
    lt = lax.dot_general(rw_ref[...], hn.astype(BF16), (((1,), (1,)), ((), ())),
                         preferred_element_type=F32)
    sub = lax.broadcasted_iota(I32, (rows, tr), 0)
    lt = jnp.where(sub < N_EXPERTS, lt, -jnp.inf)
    m1 = jnp.max(lt, axis=0, keepdims=True)
    i1 = jnp.min(jnp.where(lt == m1, sub, rows), axis=0, keepdims=True)
    lt2 = jnp.where(sub == i1, -jnp.inf, lt)
    m2 = jnp.max(lt2, axis=0, keepdims=True)
    i2 = jnp.min(jnp.where(lt2 == m2, sub, rows), axis=0, keepdims=True)
    e2 = jnp.exp(m2 - m1)
    w1 = 1.0 / (1.0 + e2)
    w2 = e2 / (1.0 + e2)
    hit = jnp.logical_or(sub == i1, sub == i2).astype(F32)
    pref = jnp.dot(hit.astype(BF16), tri_ref[...], preferred_element_type=F32)
    tile_cnt = jnp.sum(hit, axis=1, keepdims=True)
    seg_rows = jnp.floor((tile_cnt + (SEG_ALIGN - 1)) * (1.0 / SEG_ALIGN)) * SEG_ALIGN
    seg_rows = jnp.broadcast_to(seg_rows, (rows, LANES))
    below = jnp.zeros((rows, LANES), F32)
    sub_l = lax.broadcasted_iota(I32, (rows, LANES), 0)
    for e in range(N_EXPERTS - 1):
        below = below + jnp.where(sub_l > e, seg_rows[e:e + 1, :], 0.0)
    pos = pref + below[:, 0:1]
    r1 = jnp.sum(jnp.where(sub == i1, pos, 0.0), axis=0, keepdims=True)
    r2 = jnp.sum(jnp.where(sub == i2, pos, 0.0), axis=0, keepdims=True)
    before = carry_ref[...]
    lane = lax.broadcasted_iota(I32, (rows, LANES), 1)
    stat_ref[0] = jnp.where(lane == 0, seg_rows, before)
    total = before + seg_rows
    carry_ref[...] = total
    cnt_ref[...] = total
    ei = jnp.where(sub == 0, i1, jnp.where(sub == 1, i2, jnp.where(
        sub == 2, r1.astype(I32), jnp.where(sub == 3, r2.astype(I32), 0))))
    ei_ref[...] = ei[0:8]
    wrow = jnp.where(sub == 0, w1, jnp.where(sub == 1, w2, jnp.where(
        sub == 2, r1, jnp.where(sub == 3, r2, 0.0))))
    wpad = jnp.concatenate([wrow, jnp.zeros((LANES - rows, tr), F32)], axis=0)
    wcol_ref[...] = wpad.T


def _sorted_rows(tile):
    return 2 * tile + LANES


def _segment_pieces(step, slen_ref, soff_ref, sdst_ref, rows):
    top = rows.bit_length() - 1
    for e in range(N_EXPERTS):
        n = slen_ref[step * N_EXPERTS + e]
        off = soff_ref[step * N_EXPERTS + e]
        dst = sdst_ref[step * N_EXPERTS + e]
        for k in range(top, SEG_ALIGN.bit_length() - 2, -1):
            done = (n >> (k + 1)) << (k + 1)
            yield ((n & (1 << k)) != 0, pl.multiple_of(off + done, SEG_ALIGN),
                   pl.multiple_of(dst + done, SEG_ALIGN), 1 << k)


def _pad_pieces(pad_ref):
    for e in range(N_EXPERTS):
        start = pad_ref[2 * e]
        n = pad_ref[2 * e + 1]
        for k in range(MOE_ROWS.bit_length() - 2, SEG_ALIGN.bit_length() - 2, -1):
            done = (n >> (k + 1)) << (k + 1)
            yield (n & (1 << k)) != 0, pl.multiple_of(start + done, SEG_ALIGN), 1 << k


def _dispatch_kernel(slen_ref, soff_ref, sdst_ref, pad_ref, h_ref, nw_ref, ei_ref, xs_ref,
                     srt_ref, zero_ref, sem, pad_sem, *, td):
    step = pl.program_id(0)
    last = pl.num_programs(0) - 1
    slot = lax.rem(step, 2)

    def pad_copy(dst_row, rows):
        return pltpu.make_async_copy(zero_ref.at[pl.ds(0, rows)], xs_ref.at[pl.ds(dst_row, rows)], pad_sem)

    def tail_copy(j):
        return pad_copy(pl.multiple_of(pad_ref[2 * N_EXPERTS] + j * MOE_ROWS, SEG_ALIGN), MOE_ROWS)

    @pl.when(step == 0)
    def _():
        zero_ref[...] = jnp.zeros_like(zero_ref)
        for present, dst_row, rows in _pad_pieces(pad_ref):
            @pl.when(present)
            def _(dst_row=dst_row, rows=rows):
                pad_copy(dst_row, rows).start()

        def start_tail(j, carry):
            tail_copy(j).start()
            return carry

        lax.fori_loop(0, pad_ref[2 * N_EXPERTS + 1], start_tail, 0)

    hn = _rms(h_ref[...], nw_ref[...]).astype(BF16)
    row = lax.broadcasted_iota(I32, (_sorted_rows(td), td), 0)
    pick = jnp.logical_or(row == ei_ref[2:3, :], row == ei_ref[3:4, :])
    srt_ref[slot] = jnp.dot(jnp.where(pick, 1.0, 0.0).astype(BF16), hn, preferred_element_type=F32)

    def copy(ring_slot, src_row, dst_row, rows):
        return pltpu.make_async_copy(srt_ref.at[ring_slot, pl.ds(src_row, rows)],
                                     xs_ref.at[pl.ds(dst_row, rows)], sem.at[ring_slot])

    def for_pieces(of_step, ring_slot, act):
        for present, src_row, dst_row, rows in _segment_pieces(of_step, slen_ref, soff_ref, sdst_ref, td):
            @pl.when(present)
            def _(src_row=src_row, dst_row=dst_row, rows=rows):
                act(copy(ring_slot, src_row, dst_row, rows))

    for_pieces(step, slot, lambda c: c.start())

    @pl.when(step > 0)
    def _():
        for_pieces(step - 1, 1 - slot, lambda c: c.wait())

    @pl.when(step == last)
    def _():
        for_pieces(step, slot, lambda c: c.wait())

    @pl.when(step == 0)
    def _():
        for present, dst_row, rows in _pad_pieces(pad_ref):
            @pl.when(present)
            def _(dst_row=dst_row, rows=rows):
                pad_copy(dst_row, rows).wait()

        def wait_tail(j, carry):
            tail_copy(j).wait()
            return carry

        lax.fori_loop(0, pad_ref[2 * N_EXPERTS + 1], wait_tail, 0)


def _stage_dispatch(seg_len, seg_off, seg_dst, pads, h, norm_w, ei, n_slots):
    t, d = h.shape
    td = _tile(t, ROW_TILE)
    return pl.pallas_call(
        functools.partial(_dispatch_kernel, td=td),
        grid_spec=pltpu.PrefetchScalarGridSpec(
            num_scalar_prefetch=4, grid=(t // td,),
            in_specs=[pl.BlockSpec((td, d), lambda i, *_: (i, 0)),
                      pl.BlockSpec((1, d), lambda i, *_: (0, 0)),
                      pl.BlockSpec((8, td), lambda i, *_: (0, i))],
            out_specs=pl.BlockSpec(memory_space=pl.ANY),
            scratch_shapes=[pltpu.VMEM((2, _sorted_rows(td), d), F32), pltpu.VMEM((MOE_ROWS, d), F32),
                            pltpu.SemaphoreType.DMA((2,)), pltpu.SemaphoreType.DMA]),
        out_shape=jax.ShapeDtypeStruct((n_slots, d), F32),
        compiler_params=pltpu.CompilerParams(dimension_semantics=("arbitrary",), has_side_effects=True,
                                             vmem_limit_bytes=VMEM_LIMIT),
        name="moe_dispatch",
    )(seg_len, seg_off, seg_dst, pads, h, norm_w.reshape(1, d), ei)


def _moe_kernel(be_ref, nu_ref, xs_ref, wg_ref, wu_ref, wd_ref, ys_ref, xb_ref, hd_ref, *, tf):
    del be_ref
    f = pl.program_id(1)

    @pl.when(pl.program_id(0) < nu_ref[0])
    def _():
        @pl.when(f == 0)
        def _():
            xb_ref[...] = xs_ref[...].astype(BF16)

        x = xb_ref[...]
        for c in range(tf // MOE_SUB):
            cs = slice(c * MOE_SUB, (c + 1) * MOE_SUB)
            g = jnp.dot(x, wg_ref[0, :, cs], preferred_element_type=F32)
            u = jnp.dot(x, wu_ref[0, :, cs], preferred_element_type=F32)
            hd_ref[:, cs] = (_silu(g) * u).astype(BF16)
        y = jnp.dot(hd_ref[...], wd_ref[0], preferred_element_type=F32)

        @pl.when(f == 0)
        def _():
            ys_ref[...] = y

        @pl.when(f > 0)
        def _():
            ys_ref[...] += y

    @pl.when(jnp.logical_and(pl.program_id(0) >= nu_ref[0], f == 0))
    def _():
        ys_ref[...] = jnp.zeros_like(ys_ref)


def _stage_moe(blk_e, n_used, xs, w_gate, w_up, w_down):
    n_slots, d = xs.shape
    d_ff = w_gate.shape[-1]
    tm = MOE_ROWS
    tf = _tile(d_ff, MOE_COLS)
    nb, nf = n_slots // tm, d_ff // tf

    def blk(b, nu):
        return jnp.minimum(b, nu[0] - 1)

    def col(b, f, nu):
        return jnp.where(b < nu[0], f, nf - 1)

    return pl.pallas_call(
        functools.partial(_moe_kernel, tf=tf),
        grid_spec=pltpu.PrefetchScalarGridSpec(
            num_scalar_prefetch=2, grid=(nb, nf),
            in_specs=[pl.BlockSpec((tm, d), lambda b, f, be, nu: (blk(b, nu), 0)),
                      pl.BlockSpec((1, d, tf), lambda b, f, be, nu: (be[blk(b, nu)], 0, col(b, f, nu))),
                      pl.BlockSpec((1, d, tf), lambda b, f, be, nu: (be[blk(b, nu)], 0, col(b, f, nu))),
                      pl.BlockSpec((1, tf, d), lambda b, f, be, nu: (be[blk(b, nu)], col(b, f, nu), 0))],
            out_specs=pl.BlockSpec((tm, d), lambda b, f, be, nu: (b, 0)),
            scratch_shapes=[pltpu.VMEM((tm, d), BF16), pltpu.VMEM((tm, tf), BF16)]),
        out_shape=jax.ShapeDtypeStruct((n_slots, d), F32),
        compiler_params=_params(("arbitrary", "arbitrary")),
        name="moe_experts",
    )(blk_e, n_used, xs, w_gate.astype(BF16), w_up.astype(BF16), w_down.astype(BF16))


def _combine_kernel(slen_ref, soff_ref, sdst_ref, h_ref, w_ref, nw_ref, ys_ref, o_ref, buf_ref, sem,
                    *, tc):
    i = pl.program_id(0)
    n = pl.num_programs(0)

    def copy(step_slot, src_row, dst_row, rows):
        return pltpu.make_async_copy(ys_ref.at[pl.ds(src_row, rows)],
                                     buf_ref.at[step_slot, pl.ds(dst_row, rows)], sem.at[step_slot])

    def gather(step, step_slot):
        for present, off, dst, rows in _segment_pieces(step, slen_ref, soff_ref, sdst_ref, tc):
            @pl.when(present)
            def _(off=off, dst=dst, rows=rows):
                copy(step_slot, dst, off, rows).start()

    @pl.when(i == 0)
    def _():
        buf_ref[:, 2 * tc:, :] = jnp.zeros((2, _sorted_rows(tc) - 2 * tc, buf_ref.shape[-1]), F32)
        gather(0, 0)

    slot = lax.rem(i, 2)

    @pl.when(i + 1 < n)
    def _():
        gather(i + 1, 1 - slot)

    for present, off, dst, rows in _segment_pieces(i, slen_ref, soff_ref, sdst_ref, tc):
        @pl.when(present)
        def _(off=off, dst=dst, rows=rows):
            copy(slot, dst, off, rows).wait()

    yb = buf_ref[slot].astype(BF16)
    pos = lax.broadcasted_iota(I32, (tc, _sorted_rows(tc)), 1)
    sel0 = jnp.where(pos == w_ref[:, 2:3].astype(I32), 1.0, 0.0).astype(BF16)
    sel1 = jnp.where(pos == w_ref[:, 3:4].astype(I32), 1.0, 0.0).astype(BF16)
    y0 = jnp.dot(sel0, yb, preferred_element_type=F32)
    y1 = jnp.dot(sel1, yb, preferred_element_type=F32)
    h = h_ref[...] + (y0 * w_ref[:, 0:1] + y1 * w_ref[:, 1:2])
    o_ref[...] = _rms(h, nw_ref[...])


def _stage_combine(seg_len, seg_off, seg_dst, h, wcol, norm_w, ys):
    t, d = h.shape
    tc = _tile(t, ROW_TILE)
    return pl.pallas_call(
        functools.partial(_combine_kernel, tc=tc),
        grid_spec=pltpu.PrefetchScalarGridSpec(
            num_scalar_prefetch=3, grid=(t // tc,),
            in_specs=[pl.BlockSpec((tc, d), lambda i, *_: (i, 0)),
                      pl.BlockSpec((tc, LANES), lambda i, *_: (i, 0)),
                      pl.BlockSpec((1, d), lambda i, *_: (0, 0)),
                      pl.BlockSpec(memory_space=pl.ANY)],
            out_specs=pl.BlockSpec((tc, d), lambda i, *_: (i, 0)),
            scratch_shapes=[pltpu.VMEM((2, _sorted_rows(tc), d), F32), pltpu.SemaphoreType.DMA((2,))]),
        out_shape=jax.ShapeDtypeStruct((t, d), F32),
        compiler_params=_params(("arbitrary",)),
        name="moe_combine",
    )(seg_len, seg_off, seg_dst, h, wcol, norm_w.reshape(1, d), ys)


def kernel(x, attn_norm_w, ffn_norm_w, a_w_in, a_conv_w, a_a_log, a_dt_bias, a_onorm_w, a_w_out,
           kv_norm_w, w_kv, b_w_q, b_sinks, b_w_o, rel_bias, ffn_w_gate, ffn_w_up, ffn_w_down,
           moe_router, moe_w_gate, moe_w_up, moe_w_down, final_norm_w):
    bsz, L, d = x.shape
    t = bsz * L

    q, k, v, gate, bg, gct = _stage_in(x, attn_norm_w[0], a_w_in[0], a_conv_w[0], a_a_log[0],
                                       a_dt_bias[0])
    og = _stage_delta(q, k, v, gate, bg, gct, a_onorm_w[0])
    h, q2, kv2 = _stage_ffn(og.reshape(t, A_WIDTH), x.reshape(t, d), a_w_out[0], ffn_norm_w[0],
                            ffn_w_gate[0], ffn_w_up[0], ffn_w_down[0], attn_norm_w[1], kv_norm_w,
                            b_w_q[0], w_kv)

    bias4 = _stage_bias(rel_bias)
    h, ei, wcol, cnt, stat = _stage_attn(q2, kv2, bias4, b_sinks[0], b_w_o[0], h.reshape(bsz, L, d),
                                         ffn_norm_w[1], moe_router[0])
    h = h.reshape(t, d)

    counts = cnt[:N_EXPERTS, 0].astype(I32)
    p_counts = (counts + MOE_ROWS - 1) // MOE_ROWS * MOE_ROWS
    p_end = jnp.cumsum(p_counts)
    p_start = p_end - p_counts
    tile_cnt = stat[:, :N_EXPERTS, 0].astype(I32)
    before = stat[:, :N_EXPERTS, 1].astype(I32)
    seg_len = tile_cnt.reshape(-1)
    seg_off = (jnp.cumsum(tile_cnt, axis=1) - tile_cnt).reshape(-1)
    seg_dst = (p_start[None, :] + before).reshape(-1)
    n_tiles = stat.shape[0]
    n_blocks = -(-(2 * t + n_tiles * N_EXPERTS * (SEG_ALIGN - 1)) // MOE_ROWS) + N_EXPERTS
    blk_e = jnp.minimum(jnp.searchsorted(p_end, jnp.arange(n_blocks, dtype=I32) * MOE_ROWS, side='right'),
                        N_EXPERTS - 1).astype(I32)
    n_used = (p_end[-1:] // MOE_ROWS).astype(I32)
    pad_start = p_start + counts
    pads = jnp.concatenate([jnp.stack([pad_start, p_end - pad_start], axis=1).reshape(-1),
                            p_end[-1:], n_blocks - n_used]).astype(I32)
    xs = _stage_dispatch(seg_len, seg_off, seg_dst, pads, h, ffn_norm_w[1], ei, n_blocks * MOE_ROWS)
    ys = _stage_moe(blk_e, n_used, xs, moe_w_gate[0], moe_w_up[0], moe_w_down[0])
    out = _stage_combine(seg_len, seg_off, seg_dst, h, wcol, final_norm_w, ys)
    return out.reshape(bsz, L, d)
```

```python
import functools
import math

import numpy as np
import jax
import jax.numpy as jnp
from jax import lax
from jax.experimental import pallas as pl
from jax.experimental.pallas import tpu as pltpu

F32 = jnp.float32
BF16 = jnp.bfloat16
I32 = jnp.int32

EPS = 1e-6
CHUNK = 64
A_HEADS = 8
A_HEAD_DIM = 128
A_WIDTH = A_HEADS * A_HEAD_DIM
A_CONV = 4
B_Q_HEADS = 16
B_KV_HEADS = 4
B_HEAD_DIM = 64
B_GROUP = B_Q_HEADS // B_KV_HEADS
WINDOW_CHUNKS = 2
BAND = (WINDOW_CHUNKS + 1) * CHUNK
REL_BUCKETS = 32
REL_MAX_DIST = 128
N_EXPERTS = 8
LANES = 128

VMEM_LIMIT = 56 * 1024 * 1024

DELTA_TILE = 1024
ROW_TILE = 512
CONV_COLS = 512
FFN_COLS = 256
MOE_ROWS = 1024
MOE_COLS = 1792
SEG_ALIGN = 8
MOE_SUB = 256
ATT_Q = 2 * CHUNK
LOCAL_CHUNKS = 2


def _tile(n, pref):
    t = min(n, pref)
    assert n % t == 0, (n, t)
    return t


def _mm(a, b):
    return jnp.dot(a.astype(BF16), b.astype(BF16), preferred_element_type=F32)


def _mm_nt(a, b):
    return lax.dot_general(a.astype(BF16), b.astype(BF16), (((1,), (1,)), ((), ())),
                           preferred_element_type=F32)


def _mm_tn(a, b):
    return lax.dot_general(a.astype(BF16), b.astype(BF16), (((0,), (0,)), ((), ())),
                           preferred_element_type=F32)


def _silu(x):
    return x * jax.nn.sigmoid(x)


def _rms(x, w):
    return x * lax.rsqrt(jnp.mean(x * x, axis=-1, keepdims=True) + EPS) * w


def _const_spec(shape):
    nd = len(shape)
    return pl.BlockSpec(shape, lambda *_: (0,) * nd, pipeline_mode=pl.Buffered(1))


def _params(sem):
    return pltpu.CompilerParams(dimension_semantics=sem, vmem_limit_bytes=VMEM_LIMIT)


def _in_kernel(x_ref, nw_ref, w_ref, wba_ref, cw_ref, ap_ref, seg_ref,
               q_ref, k_ref, v_ref, gate_ref, bg_ref, gct_ref, ext_ref, *, tm):
    @pl.when(pl.program_id(1) == 0)
    def _():
        ext_ref[0:8, :] = jnp.zeros((8, 3 * A_WIDTH), F32)

    hn = _rms(x_ref[0], nw_ref[...]).astype(BF16)

    ba = jnp.dot(hn, wba_ref[...], preferred_element_type=F32)
    beta = jax.nn.sigmoid(ba)
    z = ba + ap_ref[1:2, :]
    softplus = jnp.maximum(z, 0.0) + jnp.log1p(jnp.exp(-jnp.abs(z)))
    g = -jnp.exp(ap_ref[0:1, :]) * softplus
    g1 = g.astype(BF16)
    r1 = g - g1.astype(F32)
    g2 = r1.astype(BF16)
    g3 = (r1 - g2.astype(F32)).astype(BF16)
    seg = seg_ref[...]
    gc = (jnp.dot(seg, g1, preferred_element_type=F32) + jnp.dot(seg, g2, preferred_element_type=F32)
          + jnp.dot(seg, g3, preferred_element_type=F32))
    lane = lax.broadcasted_iota(I32, (tm, LANES), 1)
    bg = jnp.where(lane < A_HEADS, beta, gc)
    bg_ref[0] = bg
    bgt = bg.T
    for ci in range(tm // CHUNK):
        gct_ref[0, ci] = bgt[0:2 * A_HEADS, ci * CHUNK:(ci + 1) * CHUNK]

    outs = (q_ref, k_ref, v_ref)
    for c in range(3 * A_WIDTH // CONV_COLS):
        cs = slice(c * CONV_COLS, (c + 1) * CONV_COLS)
        p = jnp.dot(hn, w_ref[:, cs], preferred_element_type=F32)
        ext_ref[8:tm + 8, cs] = p
        acc = cw_ref[A_CONV - 1:A_CONV, cs] * p
        for j in range(A_CONV - 1):
            acc = acc + cw_ref[j:j + 1, cs] * ext_ref[5 + j:5 + j + tm, cs]
        ext_ref[0:8, cs] = p[tm - 8:, :]
        a = _silu(acc)
        which, off = divmod(c * CONV_COLS, A_WIDTH)
        for hh in range(CONV_COLS // A_HEAD_DIM):
            hs = a[:, hh * A_HEAD_DIM:(hh + 1) * A_HEAD_DIM]
            if which < 2:
                scale = lax.rsqrt(jnp.sum(hs * hs, axis=-1, keepdims=True) + EPS)
                hs = hs * (scale * (A_HEAD_DIM ** -0.5) if which == 0 else scale)
            lo = off + hh * A_HEAD_DIM
            outs[which][0, :, lo:lo + A_HEAD_DIM] = hs.astype(BF16)

    for c in range(A_WIDTH // CONV_COLS):
        cs = slice(3 * A_WIDTH + c * CONV_COLS, 3 * A_WIDTH + (c + 1) * CONV_COLS)
        gate_ref[0, :, c * CONV_COLS:(c + 1) * CONV_COLS] = jnp.dot(
            hn, w_ref[:, cs], preferred_element_type=F32)


def _stage_in(x, norm_w, w_in, conv_w, a_log, dt_bias):
    bsz, L, d = x.shape
    tm = _tile(L, ROW_TILE)
    nc = L // CHUNK
    w_main = w_in[:, :4 * A_WIDTH].astype(BF16)
    w_ba = jnp.pad(w_in[:, 4 * A_WIDTH:], ((0, 0), (0, LANES - 2 * A_HEADS))).astype(BF16)
    ap = jnp.zeros((2, LANES), F32)
    ap = ap.at[0, A_HEADS:2 * A_HEADS].set(a_log.astype(F32))
    ap = ap.at[1, A_HEADS:2 * A_HEADS].set(dt_bias.astype(F32))
    r = np.arange(tm)
    seg = jnp.asarray(((r[:, None] >= r[None, :]) &
                       (r[:, None] // CHUNK == r[None, :] // CHUNK)).astype(np.float32)).astype(BF16)
    row_spec = lambda w: pl.BlockSpec((1, tm, w), lambda b, i: (b, i, 0))
    act = jax.ShapeDtypeStruct((bsz, L, A_WIDTH), BF16)
    return pl.pallas_call(
        functools.partial(_in_kernel, tm=tm),
        grid=(bsz, L // tm),
        in_specs=[row_spec(d), _const_spec((1, d)), _const_spec((d, 4 * A_WIDTH)),
                  _const_spec((d, LANES)), _const_spec((A_CONV, 3 * A_WIDTH)),
                  _const_spec((2, LANES)), _const_spec((tm, tm))],
        out_specs=[row_spec(A_WIDTH), row_spec(A_WIDTH), row_spec(A_WIDTH), row_spec(A_WIDTH),
                   row_spec(LANES),
                   pl.BlockSpec((1, tm // CHUNK, 2 * A_HEADS, CHUNK), lambda b, i: (b, i, 0, 0))],
        out_shape=[act, act, act, jax.ShapeDtypeStruct((bsz, L, A_WIDTH), F32),
                   jax.ShapeDtypeStruct((bsz, L, LANES), F32),
                   jax.ShapeDtypeStruct((bsz, nc, 2 * A_HEADS, CHUNK), F32)],
        scratch_shapes=[pltpu.VMEM((tm + 8, 3 * A_WIDTH), F32)],
        compiler_params=_params(("arbitrary", "arbitrary")),
        name="in_proj",
    )(x, norm_w.reshape(1, d), w_main, w_ba, conv_w.astype(F32), ap, seg)


def _chunk_rows(c):
    if isinstance(c, int):
        return slice(c * CHUNK, (c + 1) * CHUNK)
    return pl.ds(pl.multiple_of(c * CHUNK, CHUNK), CHUNK)


def _delta_kernel(q_ref, k_ref, v_ref, gate_ref, bg_ref, gct_ref, ow_ref, o_ref,
                  s_ref, u_ref, w_ref, qd_ref, kd_ref, at_ref, *, cb):
    @pl.when(pl.program_id(1) == 0)
    def _():
        s_ref[...] = jnp.zeros_like(s_ref)

    heads = range(A_HEADS)
    hcols = [slice(h * A_HEAD_DIM, (h + 1) * A_HEAD_DIM) for h in heads]
    row = lax.broadcasted_iota(I32, (CHUNK, CHUNK), 0)
    col = lax.broadcasted_iota(I32, (CHUNK, CHUNK), 1)
    tri = row >= col
    strict = row > col
    diag = (row >> 4) == (col >> 4)
    eye = (row == col).astype(F32)

    def local(cc):
        items = [(j, h) for j in range(LOCAL_CHUNKS) for h in heads]
        its = range(len(items))
        cidx = [cc * LOCAL_CHUNKS + j for j in range(LOCAL_CHUNKS)]
        crow = [_chunk_rows(c) for c in cidx]
        rows = [crow[j] for j, _ in items]
        cols = [hcols[h] for _, h in items]
        qb = [q_ref[0, rows[i], cols[i]] for i in its]
        kb = [k_ref[0, rows[i], cols[i]] for i in its]
        vb = [v_ref[0, rows[i], cols[i]] for i in its]
        bgs = [bg_ref[0, crow[j], :] for j in range(LOCAL_CHUNKS)]
        gts = [gct_ref[0, c] for c in cidx]
        beta = [bgs[j][:, h:h + 1] for j, h in items]
        gcol = [bgs[j][:, A_HEADS + h:A_HEADS + h + 1] for j, h in items]
        grow = [gts[j][A_HEADS + h:A_HEADS + h + 1, :] for j, h in items]

        decay = [jnp.exp(jnp.where(tri, gcol[i] - grow[i], -jnp.inf)) for i in its]
        eg = [jnp.exp(gcol[i]) for i in its]
        kf = [kb[i].astype(F32) for i in its]
        kbeta = [kf[i] * beta[i] for i in its]
        kq = [_mm_nt(jnp.concatenate([kbeta[i].astype(BF16), qb[i]], axis=0), kb[i]) for i in its]
        lower = [jnp.where(strict, kq[i][:CHUNK] * decay[i], 0.0) for i in its]
        ld = [jnp.where(diag, lower[i], 0.0) for i in its]
        lo = [lower[i] - ld[i] for i in its]
        p = [eye - ld[i] for i in its]
        s = [_mm(ld[i], ld[i]) for i in its]
        for _ in range(2):
            ps = [_mm(jnp.concatenate([p[i], s[i]], axis=0), s[i]) for i in its]
            p = [p[i] + ps[i][:CHUNK] for i in its]
            s = [ps[i][CHUNK:] for i in its]
        td = [p[i] + _mm(p[i], s[i]) for i in its]
        rhs = [jnp.concatenate([vb[i].astype(F32) * beta[i], kbeta[i] * eg[i]], axis=1) for i in its]
        r = [_mm(td[i], rhs[i]) for i in its]
        m = [_mm(td[i], lo[i]) for i in its]
        m2 = [_mm(m[i], m[i]) for i in its]
        r = [r[i] + _mm(m2[i], r[i]) for i in its]
        uw = [r[i] - _mm(m[i], r[i]) for i in its]
        attn = [kq[i][CHUNK:] * decay[i] for i in its]
        qd = [qb[i].astype(F32) * eg[i] for i in its]
        kd = [kf[i] * jnp.exp(grow[i][:, CHUNK - 1:CHUNK] - gcol[i]) for i in its]

        def store():
            for i, (_, h) in enumerate(items):
                u_ref[rows[i], cols[i]] = uw[i][:, :A_HEAD_DIM]
                w_ref[rows[i], cols[i]] = uw[i][:, A_HEAD_DIM:].astype(BF16)
                qd_ref[rows[i], cols[i]] = qd[i].astype(BF16)
                kd_ref[rows[i], cols[i]] = kd[i].astype(BF16)
                at_ref[h, rows[i], :] = attn[i].astype(BF16)

        return store

    def recur(cc):
        cidx = [cc * LOCAL_CHUNKS + j for j in range(LOCAL_CHUNKS)]
        crow = [_chunk_rows(c) for c in cidx]
        wq = [[jnp.concatenate([w_ref[r, hcols[h]], qd_ref[r, hcols[h]]], axis=0) for h in heads]
              for r in crow]
        u = [[u_ref[r, hcols[h]] for h in heads] for r in crow]
        kd = [[kd_ref[r, hcols[h]] for h in heads] for r in crow]
        at = [[at_ref[h, r, :] for h in heads] for r in crow]
        gate = [[gate_ref[0, r, hcols[h]] for h in heads] for r in crow]
        gts = [gct_ref[0, c] for c in cidx]
        st = [s_ref[h] for h in heads]
        ow = ow_ref[...]

        og = []
        for j in range(LOCAL_CHUNKS):
            eglast = [jnp.exp(gts[j][A_HEADS + h:A_HEADS + h + 1, CHUNK - 1:CHUNK]) for h in heads]
            ws_qs = [_mm(wq[j][h], st[h]) for h in heads]
            vnew = [u[j][h] - ws_qs[h][:CHUNK] for h in heads]
            av = [_mm(jnp.concatenate([at[j][h], kd[j][h].T], axis=0), vnew[h]) for h in heads]
            o = [ws_qs[h][CHUNK:] + av[h][:CHUNK] for h in heads]
            st = [st[h] * eglast[h] + av[h][CHUNK:] for h in heads]
            og.append([_rms(o[h], ow) * _silu(gate[j][h]) for h in heads])

        def store():
            for h in heads:
                s_ref[h] = st[h]
                for j in range(LOCAL_CHUNKS):
                    o_ref[0, crow[j], hcols[h]] = og[j][h].astype(BF16)

        return store

    def fused(cc, carry):
        store_local = local(cc + 1)
        store_recur = recur(cc)
        store_local()
        store_recur()
        return carry

    groups = cb // LOCAL_CHUNKS
    local(0)()
    lax.fori_loop(0, groups - 1, fused, 0)
    recur(groups - 1)()


def _stage_delta(q, k, v, gate, bg, gct, onorm_w):
    bsz, L, _ = q.shape
    tm = _tile(L, DELTA_TILE)
    cb = tm // CHUNK
    row_spec = lambda w: pl.BlockSpec((1, tm, w), lambda b, i: (b, i, 0))
    return pl.pallas_call(
        functools.partial(_delta_kernel, cb=cb),
        grid=(bsz, L // tm),
        in_specs=[row_spec(A_WIDTH), row_spec(A_WIDTH), row_spec(A_WIDTH), row_spec(A_WIDTH),
                  row_spec(LANES),
                  pl.BlockSpec((1, cb, 2 * A_HEADS, CHUNK), lambda b, i: (b, i, 0, 0)),
                  _const_spec((1, A_HEAD_DIM))],
        out_specs=row_spec(A_WIDTH),
        out_shape=jax.ShapeDtypeStruct((bsz, L, A_WIDTH), BF16),
        scratch_shapes=[pltpu.VMEM((A_HEADS, A_HEAD_DIM, A_HEAD_DIM), F32),
                        pltpu.VMEM((tm, A_WIDTH), F32), pltpu.VMEM((tm, A_WIDTH), BF16),
                        pltpu.VMEM((tm, A_WIDTH), BF16), pltpu.VMEM((tm, A_WIDTH), BF16),
                        pltpu.VMEM((A_HEADS, tm, CHUNK), BF16)],
        compiler_params=_params(("arbitrary", "arbitrary")),
        name="delta_rule",
    )(q, k, v, gate, bg, gct, onorm_w.reshape(1, A_HEAD_DIM).astype(F32))


def _ffn_kernel(og_ref, x_ref, wo_ref, fw_ref, wg_ref, wu_ref, wd_ref, aw_ref, kw_ref, wq_ref,
                wkv_ref, h_ref, q_ref, kv_ref, hd_ref, *, d_ff):
    h1 = x_ref[...] + jnp.dot(og_ref[...], wo_ref[...], preferred_element_type=F32)
    hn = _rms(h1, fw_ref[...]).astype(BF16)
    for c in range(d_ff // FFN_COLS):
        cs = slice(c * FFN_COLS, (c + 1) * FFN_COLS)
        g = jnp.dot(hn, wg_ref[:, cs], preferred_element_type=F32)
        u = jnp.dot(hn, wu_ref[:, cs], preferred_element_type=F32)
        hd_ref[:, cs] = (_silu(g) * u).astype(BF16)
    h2 = h1 + jnp.dot(hd_ref[...], wd_ref[...], preferred_element_type=F32)
    h_ref[...] = h2
    xh = h2 * lax.rsqrt(jnp.mean(h2 * h2, axis=-1, keepdims=True) + EPS)
    q = jnp.dot((xh * aw_ref[...]).astype(BF16), wq_ref[...], preferred_element_type=F32)
    q_ref[...] = (q * (B_HEAD_DIM ** -0.5)).astype(BF16)
    kv_ref[...] = jnp.dot((xh * kw_ref[...]).astype(BF16), wkv_ref[...],
                          preferred_element_type=F32).astype(BF16)


def _stage_ffn(og, x, w_out, ffn_norm_w, w_gate, w_up, w_down, attn_norm_w, kv_norm_w, w_q, w_kv):
    t, d = x.shape
    d_ff = w_gate.shape[1]
    tm = _tile(t, ROW_TILE)
    hq = B_Q_HEADS * B_HEAD_DIM
    half = B_KV_HEADS * B_HEAD_DIM
    dup = lambda w: jnp.concatenate([w.reshape(d, B_KV_HEADS, 1, B_HEAD_DIM)] * 2, axis=2).reshape(d, 2 * half)
    w_kvd = jnp.concatenate([dup(w_kv[:, :half]), dup(w_kv[:, half:])], axis=1).astype(BF16)
    row_spec = lambda w: pl.BlockSpec((tm, w), lambda i: (i, 0))
    return pl.pallas_call(
        functools.partial(_ffn_kernel, d_ff=d_ff),
        grid=(t // tm,),
        in_specs=[row_spec(A_WIDTH), row_spec(d), _const_spec((A_WIDTH, d)), _const_spec((1, d)),
                  _const_spec((d, d_ff)), _const_spec((d, d_ff)), _const_spec((d_ff, d)),
                  _const_spec((1, d)), _const_spec((1, d)), _const_spec((d, hq)),
                  _const_spec((d, 4 * half))],
        out_specs=[row_spec(d), row_spec(hq), row_spec(4 * half)],
        out_shape=[jax.ShapeDtypeStruct((t, d), F32), jax.ShapeDtypeStruct((t, hq), BF16),
                   jax.ShapeDtypeStruct((t, 4 * half), BF16)],
        scratch_shapes=[pltpu.VMEM((tm, d_ff), BF16)],
        compiler_params=_params(("arbitrary",)),
        name="ffn_dense",
    )(og, x, w_out.astype(BF16), ffn_norm_w.reshape(1, d), w_gate.astype(BF16), w_up.astype(BF16),
      w_down.astype(BF16), attn_norm_w.reshape(1, d), kv_norm_w.reshape(1, d), w_q.astype(BF16), w_kvd)


def _rel_buckets(rel):
    nb = REL_BUCKETS // 2
    max_exact = nb // 2
    ret = jnp.where(rel > 0, nb, 0)
    dist = jnp.abs(rel)
    dist_f = jnp.maximum(dist, 1).astype(F32)
    large = max_exact + (jnp.log(dist_f / max_exact) / math.log(REL_MAX_DIST / max_exact)
                         * (nb - max_exact)).astype(I32)
    large = jnp.minimum(large, nb - 1)
    return ret + jnp.where(dist < max_exact, dist, large)


def _bias_kernel(rb_ref, bk_ref, o_ref):
    bk = bk_ref[...]
    for h in range(B_Q_HEADS):
        acc = jnp.zeros(bk.shape, F32)
        for b in range(REL_BUCKETS):
            acc = jnp.where(bk == b, rb_ref[b * B_Q_HEADS + h], acc)
        o_ref[h] = acc


def _stage_bias(rel_bias):
    rel = (jnp.arange(BAND)[None, :] - WINDOW_CHUNKS * CHUNK) - jnp.arange(CHUNK)[:, None]
    buckets = _rel_buckets(rel).astype(I32)
    bias = pl.pallas_call(
        _bias_kernel,
        in_specs=[pl.BlockSpec(memory_space=pltpu.SMEM), pl.BlockSpec(memory_space=pltpu.VMEM)],
        out_specs=pl.BlockSpec(memory_space=pltpu.VMEM),
        out_shape=jax.ShapeDtypeStruct((B_Q_HEADS, CHUNK, BAND), F32),
        name="rel_bias",
    )(rel_bias.astype(F32).reshape(-1), buckets)
    pad = lambda lo, hi: jnp.pad(bias, ((0, 0), (0, 0), (lo, hi)), constant_values=-jnp.inf)
    pair = jnp.concatenate([pad(0, CHUNK), pad(CHUNK, 0)], axis=1)
    return pair.reshape(B_KV_HEADS, B_GROUP * ATT_Q, 2 * ATT_Q)


def _attn_kernel(q_ref, kv_ref, kvp_ref, bias_ref, sink_ref, wo_ref, h_ref, nw_ref, rw_ref, tri_ref,
                 o_ref, ei_ref, wcol_ref, cnt_ref, stat_ref, att_ref, carry_ref, *, nblk, tq):
    first = pl.program_id(1) == 0
    lane = lax.broadcasted_iota(I32, (ATT_Q, LANES), 1)
    lo_half = lane < B_HEAD_DIM
    kidx = lax.broadcasted_iota(I32, (B_GROUP * ATT_Q, 2 * ATT_Q), 1)
    koff = B_KV_HEADS * LANES
    for jb in range(nblk):
        rows = slice(jb * ATT_Q, (jb + 1) * ATT_Q)
        if jb == 0:
            keys = jnp.concatenate([kvp_ref[0], kv_ref[0, :ATT_Q, :]], axis=0)
        else:
            keys = kv_ref[0, (jb - 1) * ATT_Q:(jb + 1) * ATT_Q, :]
        for hk in range(B_KV_HEADS):
            kd = keys[:, hk * LANES:(hk + 1) * LANES]
            vd = keys[:, koff + hk * LANES:koff + (hk + 1) * LANES]
            qa = q_ref[0, rows, (2 * hk) * LANES:(2 * hk + 1) * LANES]
            qb = q_ref[0, rows, (2 * hk + 1) * LANES:(2 * hk + 2) * LANES]
            zero = jnp.zeros_like(qa)
            q4 = jnp.concatenate([jnp.where(lo_half, qa, zero), jnp.where(lo_half, zero, qa),
                                  jnp.where(lo_half, qb, zero), jnp.where(lo_half, zero, qb)], axis=0)
            s = lax.dot_general(q4, kd, (((1,), (1,)), ((), ())), preferred_element_type=F32)
            s = s + bias_ref[hk]
            if jb == 0:
                s = jnp.where(jnp.logical_and(first, kidx < ATT_Q), -jnp.inf, s)
            sink = sink_ref[hk]
            m = jnp.maximum(jnp.max(s, axis=-1, keepdims=True), sink)
            p = jnp.exp(s - m)
            denom = jnp.sum(p, axis=-1, keepdims=True) + jnp.exp(sink - m)
            o = jnp.dot(p.astype(BF16), vd, preferred_element_type=F32) / denom
            ta = jnp.where(lo_half, o[0:ATT_Q], o[ATT_Q:2 * ATT_Q])
            tb = jnp.where(lo_half, o[2 * ATT_Q:3 * ATT_Q], o[3 * ATT_Q:4 * ATT_Q])
            att_ref[rows, (2 * hk) * LANES:(2 * hk + 1) * LANES] = ta.astype(BF16)
            att_ref[rows, (2 * hk + 1) * LANES:(2 * hk + 2) * LANES] = tb.astype(BF16)
    h_out = h_ref[0] + jnp.dot(att_ref[...], wo_ref[...], preferred_element_type=F32)
    o_ref[0] = h_out
    _route(h_out, jnp.logical_and(pl.program_id(0) == 0, first), nw_ref, rw_ref, tri_ref,
           ei_ref, wcol_ref, cnt_ref, stat_ref, carry_ref, tq)


def _stage_attn(q, kv, bias4, sinks, w_o, h, ffn_norm_w, router_w):
    bsz, L, d = h.shape
    tq = _tile(L, ROW_TILE)
    nt = L // tq
    halo = WINDOW_CHUNKS * CHUNK
    per = tq // halo
    hq = B_Q_HEADS * B_HEAD_DIM
    sink4 = jnp.repeat(sinks.astype(F32).reshape(B_KV_HEADS, B_GROUP), ATT_Q, axis=1)[..., None]
    rows = 2 * N_EXPERTS
    rwt = jnp.pad(router_w.T, ((0, rows - N_EXPERTS), (0, 0))).astype(BF16)
    r = np.arange(tq)
    tri = jnp.asarray((r[:, None] < r[None, :]).astype(np.float32)).astype(BF16)
    row_spec = lambda w: pl.BlockSpec((1, tq, w), lambda b, i: (b, i, 0))
    return pl.pallas_call(
        functools.partial(_attn_kernel, nblk=tq // ATT_Q, tq=tq),
        grid=(bsz, nt),
        in_specs=[row_spec(hq), row_spec(kv.shape[-1]),
                  pl.BlockSpec((1, halo, kv.shape[-1]), lambda b, i: (b, jnp.maximum(i * per - 1, 0), 0)),
                  _const_spec(bias4.shape), _const_spec(sink4.shape), _const_spec((hq, d)),
                  row_spec(d), _const_spec((1, d)), _const_spec((rows, d)), _const_spec((tq, tq))],
        out_specs=[row_spec(d),
                   pl.BlockSpec((8, tq), lambda b, i: (0, b * nt + i)),
                   pl.BlockSpec((tq, LANES), lambda b, i: (b * nt + i, 0)),
                   pl.BlockSpec((rows, LANES), lambda b, i: (0, 0)),
                   pl.BlockSpec((1, rows, LANES), lambda b, i: (b * nt + i, 0, 0))],
        out_shape=[jax.ShapeDtypeStruct((bsz, L, d), F32),
                   jax.ShapeDtypeStruct((8, bsz * L), I32),
                   jax.ShapeDtypeStruct((bsz * L, LANES), F32),
                   jax.ShapeDtypeStruct((rows, LANES), F32),
                   jax.ShapeDtypeStruct((bsz * nt, rows, LANES), F32)],
        scratch_shapes=[pltpu.VMEM((tq, hq), BF16), pltpu.VMEM((rows, LANES), F32)],
        compiler_params=_params(("arbitrary", "arbitrary")),
        name="swa_attn",
    )(q.reshape(bsz, L, hq), kv.reshape(bsz, L, -1), kv.reshape(bsz, L, -1), bias4, sink4,
      w_o.astype(BF16), h, ffn_norm_w.reshape(1, d), rwt, tri)


def _route(h, first, nw_ref, rw_ref, tri_ref, ei_ref, wcol_ref, cnt_ref, stat_ref, carry_ref, tr):
    @pl.when(first)
    def _():
        carry_ref[...] = jnp.zeros_like(carry_ref)

    hn = _rms(h, nw_ref[...])
    rows = 2 * N_EXPERTS
    lt = lax.dot_general(rw_ref[...], hn.astype(BF16), (((1,), (1,)), ((), ())),
                         preferred_element_type=F32)
    sub = lax.broadcasted_iota(I32, (rows, tr), 0)
    lt = jnp.where(sub < N_EXPERTS, lt, -jnp.inf)
    m1 = jnp.max(lt, axis=0, keepdims=True)
    i1 = jnp.min(jnp.where(lt == m1, sub, rows), axis=0, keepdims=True)
    lt2 = jnp.where(sub == i1, -jnp.inf, lt)
    m2 = jnp.max(lt2, axis=0, keepdims=True)
    i2 = jnp.min(jnp.where(lt2 == m2, sub, rows), axis=0, keepdims=True)
    e2 = jnp.exp(m2 - m1)
    w1 = 1.0 / (1.0 + e2)
    w2 = e2 / (1.0 + e2)
    hit = jnp.logical_or(sub == i1, sub == i2).astype(F32)
    pref = jnp.dot(hit.astype(BF16), tri_ref[...], preferred_element_type=F32)
    tile_cnt = jnp.sum(hit, axis=1, keepdims=True)
    seg_rows = jnp.floor((tile_cnt + (SEG_ALIGN - 1)) * (1.0 / SEG_ALIGN)) * SEG_ALIGN
    seg_rows = jnp.broadcast_to(seg_rows, (rows, LANES))
    below = jnp.zeros((rows, LANES), F32)
    sub_l = lax.broadcasted_iota(I32, (rows, LANES), 0)
    for e in range(N_EXPERTS - 1):
        below = below + jnp.where(sub_l > e, seg_rows[e:e + 1, :], 0.0)
    pos = pref + below[:, 0:1]
    r1 = jnp.sum(jnp.where(sub == i1, pos, 0.0), axis=0, keepdims=True)
    r2 = jnp.sum(jnp.where(sub == i2, pos, 0.0), axis=0, keepdims=True)
    before = carry_ref[...]
    lane = lax.broadcasted_iota(I32, (rows, LANES), 1)
    stat_ref[0] = jnp.where(lane == 0, seg_rows, before)
    total = before + seg_rows
    carry_ref[...] = total
    cnt_ref[...] = total
    ei = jnp.where(sub == 0, i1, jnp.where(sub == 1, i2, jnp.where(
        sub == 2, r1.astype(I32), jnp.where(sub == 3, r2.astype(I32), 0))))
    ei_ref[...] = ei[0:8]
    wrow = jnp.where(sub == 0, w1, jnp.where(sub == 1, w2, jnp.where(
        sub == 2, r1, jnp.where(sub == 3, r2, 0.0))))
    wpad = jnp.concatenate([wrow, jnp.zeros((LANES - rows, tr), F32)], axis=0)
    wcol_ref[...] = wpad.T


def _sorted_rows(tile):
    return 2 * tile + LANES


def _segment_pieces(step, slen_ref, soff_ref, sdst_ref, rows):
    top = rows.bit_length() - 1
    for e in range(N_EXPERTS):
        n = slen_ref[step * N_EXPERTS + e]
        off = soff_ref[step * N_EXPERTS + e]
        dst = sdst_ref[step * N_EXPERTS + e]
        for k in range(top, SEG_ALIGN.bit_length() - 2, -1):
            done = (n >> (k + 1)) << (k + 1)
            yield ((n & (1 << k)) != 0, pl.multiple_of(off + done, SEG_ALIGN),
                   pl.multiple_of(dst + done, SEG_ALIGN), 1 << k)


def _pad_pieces(pad_ref):
    for e in range(N_EXPERTS):
        start = pad_ref[2 * e]
        n = pad_ref[2 * e + 1]
        for k in range(MOE_ROWS.bit_length() - 2, SEG_ALIGN.bit_length() - 2, -1):
            done = (n >> (k + 1)) << (k + 1)
            yield (n & (1 << k)) != 0, pl.multiple_of(start + done, SEG_ALIGN), 1 << k


def _dispatch_kernel(slen_ref, soff_ref, sdst_ref, pad_ref, h_ref, nw_ref, ei_ref, xs_ref,
                     srt_ref, zero_ref, sem, pad_sem, *, td):
    step = pl.program_id(0)
    last = pl.num_programs(0) - 1
    slot = lax.rem(step, 2)

    def pad_copy(dst_row, rows):
        return pltpu.make_async_copy(zero_ref.at[pl.ds(0, rows)], xs_ref.at[pl.ds(dst_row, rows)], pad_sem)

    def tail_copy(j):
        return pad_copy(pl.multiple_of(pad_ref[2 * N_EXPERTS] + j * MOE_ROWS, SEG_ALIGN), MOE_ROWS)

    @pl.when(step == 0)
    def _():
        zero_ref[...] = jnp.zeros_like(zero_ref)
        for present, dst_row, rows in _pad_pieces(pad_ref):
            @pl.when(present)
            def _(dst_row=dst_row, rows=rows):
                pad_copy(dst_row, rows).start()

        def start_tail(j, carry):
            tail_copy(j).start()
            return carry

        lax.fori_loop(0, pad_ref[2 * N_EXPERTS + 1], start_tail, 0)

    hn = _rms(h_ref[...], nw_ref[...]).astype(BF16)
    row = lax.broadcasted_iota(I32, (_sorted_rows(td), td), 0)
    pick = jnp.logical_or(row == ei_ref[2:3, :], row == ei_ref[3:4, :])
    srt_ref[slot] = jnp.dot(jnp.where(pick, 1.0, 0.0).astype(BF16), hn, preferred_element_type=F32)

    def copy(ring_slot, src_row, dst_row, rows):
        return pltpu.make_async_copy(srt_ref.at[ring_slot, pl.ds(src_row, rows)],
                                     xs_ref.at[pl.ds(dst_row, rows)], sem.at[ring_slot])

    def for_pieces(of_step, ring_slot, act):
        for present, src_row, dst_row, rows in _segment_pieces(of_step, slen_ref, soff_ref, sdst_ref, td):
            @pl.when(present)
            def _(src_row=src_row, dst_row=dst_row, rows=rows):
                act(copy(ring_slot, src_row, dst_row, rows))

    for_pieces(step, slot, lambda c: c.start())

    @pl.when(step > 0)
    def _():
        for_pieces(step - 1, 1 - slot, lambda c: c.wait())

    @pl.when(step == last)
    def _():
        for_pieces(step, slot, lambda c: c.wait())

    @pl.when(step == 0)
    def _():
        for present, dst_row, rows in _pad_pieces(pad_ref):
            @pl.when(present)
            def _(dst_row=dst_row, rows=rows):
                pad_copy(dst_row, rows).wait()

        def wait_tail(j, carry):
            tail_copy(j).wait()
            return carry

        lax.fori_loop(0, pad_ref[2 * N_EXPERTS + 1], wait_tail, 0)


def _stage_dispatch(seg_len, seg_off, seg_dst, pads, h, norm_w, ei, n_slots):
    t, d = h.shape
    td = _tile(t, ROW_TILE)
    return pl.pallas_call(
        functools.partial(_dispatch_kernel, td=td),
        grid_spec=pltpu.PrefetchScalarGridSpec(
            num_scalar_prefetch=4, grid=(t // td,),
            in_specs=[pl.BlockSpec((td, d), lambda i, *_: (i, 0)),
                      pl.BlockSpec((1, d), lambda i, *_: (0, 0)),
                      pl.BlockSpec((8, td), lambda i, *_: (0, i))],
            out_specs=pl.BlockSpec(memory_space=pl.ANY),
            scratch_shapes=[pltpu.VMEM((2, _sorted_rows(td), d), F32), pltpu.VMEM((MOE_ROWS, d), F32),
                            pltpu.SemaphoreType.DMA((2,)), pltpu.SemaphoreType.DMA]),
        out_shape=jax.ShapeDtypeStruct((n_slots, d), F32),
        compiler_params=pltpu.CompilerParams(dimension_semantics=("arbitrary",), has_side_effects=True,
                                             vmem_limit_bytes=VMEM_LIMIT),
        name="moe_dispatch",
    )(seg_len, seg_off, seg_dst, pads, h, norm_w.reshape(1, d), ei)


def _moe_kernel(be_ref, nu_ref, xs_ref, wg_ref, wu_ref, wd_ref, ys_ref, xb_ref, hd_ref, *, tf):
    del be_ref
    f = pl.program_id(1)

    used = pl.program_id(0) < nu_ref[0]

    def step(first):
        if first:
            xb_ref[...] = xs_ref[...].astype(BF16)
        x = xb_ref[...]
        for c in range(tf // MOE_SUB):
            cs = slice(c * MOE_SUB, (c + 1) * MOE_SUB)
            g = jnp.dot(x, wg_ref[0, :, cs], preferred_element_type=F32)
            u = jnp.dot(x, wu_ref[0, :, cs], preferred_element_type=F32)
            hd_ref[:, cs] = (_silu(g) * u).astype(BF16)
        d = ys_ref.shape[-1]
        for n in range(d // MOE_SUB):
            ns = slice(n * MOE_SUB, (n + 1) * MOE_SUB)
            y = jnp.dot(hd_ref[...], wd_ref[0, :, ns], preferred_element_type=F32)
            if first:
                ys_ref[:, ns] = y
            else:
                ys_ref[:, ns] += y

    @pl.when(jnp.logical_and(used, f == 0))
    def _():
        step(True)

    @pl.when(jnp.logical_and(used, f > 0))
    def _():
        step(False)

    @pl.when(jnp.logical_and(pl.program_id(0) >= nu_ref[0], f == 0))
    def _():
        ys_ref[...] = jnp.zeros_like(ys_ref)


def _stage_moe(blk_e, n_used, xs, w_gate, w_up, w_down):
    n_slots, d = xs.shape
    d_ff = w_gate.shape[-1]
    tm = MOE_ROWS
    tf = _tile(d_ff, MOE_COLS)
    nb, nf = n_slots // tm, d_ff // tf

    def blk(b, nu):
        return jnp.minimum(b, nu[0] - 1)

    def col(b, f, nu):
        return jnp.where(b < nu[0], f, nf - 1)

    return pl.pallas_call(
        functools.partial(_moe_kernel, tf=tf),
        grid_spec=pltpu.PrefetchScalarGridSpec(
            num_scalar_prefetch=2, grid=(nb, nf),
            in_specs=[pl.BlockSpec((tm, d), lambda b, f, be, nu: (blk(b, nu), 0)),
                      pl.BlockSpec((1, d, tf), lambda b, f, be, nu: (be[blk(b, nu)], 0, col(b, f, nu))),
                      pl.BlockSpec((1, d, tf), lambda b, f, be, nu: (be[blk(b, nu)], 0, col(b, f, nu))),
                      pl.BlockSpec((1, tf, d), lambda b, f, be, nu: (be[blk(b, nu)], col(b, f, nu), 0))],
            out_specs=pl.BlockSpec((tm, d), lambda b, f, be, nu: (b, 0)),
            scratch_shapes=[pltpu.VMEM((tm, d), BF16), pltpu.VMEM((tm, tf), BF16)]),
        out_shape=jax.ShapeDtypeStruct((n_slots, d), F32),
        compiler_params=_params(("arbitrary", "arbitrary")),
        name="moe_experts",
    )(blk_e, n_used, xs, w_gate.astype(BF16), w_up.astype(BF16), w_down.astype(BF16))


def _combine_kernel(slen_ref, soff_ref, sdst_ref, h_ref, w_ref, nw_ref, ys_ref, o_ref, buf_ref, sem,
                    *, tc):
    i = pl.program_id(0)
    n = pl.num_programs(0)

    def copy(step_slot, src_row, dst_row, rows):
        return pltpu.make_async_copy(ys_ref.at[pl.ds(src_row, rows)],
                                     buf_ref.at[step_slot, pl.ds(dst_row, rows)], sem.at[step_slot])

    def gather(step, step_slot):
        for present, off, dst, rows in _segment_pieces(step, slen_ref, soff_ref, sdst_ref, tc):
            @pl.when(present)
            def _(off=off, dst=dst, rows=rows):
                copy(step_slot, dst, off, rows).start()

    @pl.when(i == 0)
    def _():
        buf_ref[:, 2 * tc:, :] = jnp.zeros((2, _sorted_rows(tc) - 2 * tc, buf_ref.shape[-1]), F32)
        gather(0, 0)

    slot = lax.rem(i, 2)

    @pl.when(i + 1 < n)
    def _():
        gather(i + 1, 1 - slot)

    for present, off, dst, rows in _segment_pieces(i, slen_ref, soff_ref, sdst_ref, tc):
        @pl.when(present)
        def _(off=off, dst=dst, rows=rows):
            copy(slot, dst, off, rows).wait()

    yb = buf_ref[slot].astype(BF16)
    pos = lax.broadcasted_iota(I32, (tc, _sorted_rows(tc)), 1)
    sel0 = jnp.where(pos == w_ref[:, 2:3].astype(I32), 1.0, 0.0).astype(BF16)
    sel1 = jnp.where(pos == w_ref[:, 3:4].astype(I32), 1.0, 0.0).astype(BF16)
    y0 = jnp.dot(sel0, yb, preferred_element_type=F32)
    y1 = jnp.dot(sel1, yb, preferred_element_type=F32)
    h = h_ref[...] + (y0 * w_ref[:, 0:1] + y1 * w_ref[:, 1:2])
    o_ref[...] = _rms(h, nw_ref[...])


def _stage_combine(seg_len, seg_off, seg_dst, h, wcol, norm_w, ys):
    t, d = h.shape
    tc = _tile(t, ROW_TILE)
    return pl.pallas_call(
        functools.partial(_combine_kernel, tc=tc),
        grid_spec=pltpu.PrefetchScalarGridSpec(
            num_scalar_prefetch=3, grid=(t // tc,),
            in_specs=[pl.BlockSpec((tc, d), lambda i, *_: (i, 0)),
                      pl.BlockSpec((tc, LANES), lambda i, *_: (i, 0)),
                      pl.BlockSpec((1, d), lambda i, *_: (0, 0)),
                      pl.BlockSpec(memory_space=pl.ANY)],
            out_specs=pl.BlockSpec((tc, d), lambda i, *_: (i, 0)),
            scratch_shapes=[pltpu.VMEM((2, _sorted_rows(tc), d), F32), pltpu.SemaphoreType.DMA((2,))]),
        out_shape=jax.ShapeDtypeStruct((t, d), F32),
        compiler_params=_params(("arbitrary",)),
        name="moe_combine",
    )(seg_len, seg_off, seg_dst, h, wcol, norm_w.reshape(1, d), ys)


def kernel(x, attn_norm_w, ffn_norm_w, a_w_in, a_conv_w, a_a_log, a_dt_bias, a_onorm_w, a_w_out,
           kv_norm_w, w_kv, b_w_q, b_sinks, b_w_o, rel_bias, ffn_w_gate, ffn_w_up, ffn_w_down,
           moe_router, moe_w_gate, moe_w_up, moe_w_down, final_norm_w):
    bsz, L, d = x.shape
    t = bsz * L

    q, k, v, gate, bg, gct = _stage_in(x, attn_norm_w[0], a_w_in[0], a_conv_w[0], a_a_log[0],
                                       a_dt_bias[0])
    og = _stage_delta(q, k, v, gate, bg, gct, a_onorm_w[0])
    h, q2, kv2 = _stage_ffn(og.reshape(t, A_WIDTH), x.reshape(t, d), a_w_out[0], ffn_norm_w[0],
                            ffn_w_gate[0], ffn_w_up[0], ffn_w_down[0], attn_norm_w[1], kv_norm_w,
                            b_w_q[0], w_kv)

    bias4 = _stage_bias(rel_bias)
    h, ei, wcol, cnt, stat = _stage_attn(q2, kv2, bias4, b_sinks[0], b_w_o[0], h.reshape(bsz, L, d),
                                         ffn_norm_w[1], moe_router[0])
    h = h.reshape(t, d)

    counts = cnt[:N_EXPERTS, 0].astype(I32)
    p_counts = (counts + MOE_ROWS - 1) // MOE_ROWS * MOE_ROWS
    p_end = jnp.cumsum(p_counts)
    p_start = p_end - p_counts
    tile_cnt = stat[:, :N_EXPERTS, 0].astype(I32)
    before = stat[:, :N_EXPERTS, 1].astype(I32)
    seg_len = tile_cnt.reshape(-1)
    seg_off = (jnp.cumsum(tile_cnt, axis=1) - tile_cnt).reshape(-1)
    seg_dst = (p_start[None, :] + before).reshape(-1)
    n_tiles = stat.shape[0]
    n_blocks = -(-(2 * t + n_tiles * N_EXPERTS * (SEG_ALIGN - 1)) // MOE_ROWS) + N_EXPERTS
    blk_e = jnp.minimum(jnp.searchsorted(p_end, jnp.arange(n_blocks, dtype=I32) * MOE_ROWS, side='right'),
                        N_EXPERTS - 1).astype(I32)
    n_used = (p_end[-1:] // MOE_ROWS).astype(I32)
    pad_start = p_start + counts
    pads = jnp.concatenate([jnp.stack([pad_start, p_end - pad_start], axis=1).reshape(-1),
                            p_end[-1:], n_blocks - n_used]).astype(I32)
    xs = _stage_dispatch(seg_len, seg_off, seg_dst, pads, h, ffn_norm_w[1], ei, n_blocks * MOE_ROWS)
    ys = _stage_moe(blk_e, n_used, xs, moe_w_gate[0], moe_w_up[0], moe_w_down[0])
    out = _stage_combine(seg_len, seg_off, seg_dst, h, wcol, final_norm_w, ys)
    return out.reshape(bsz, L, d)
```
